```python
import math, functools
import jax, jax.numpy as jnp
from jax import lax
import numpy as np

D_MODEL = 1024
BATCH = 2
SEQ = 8192
DEPTH = 1
DEC_BATCH = 128
DEC_SEQ = 4
PAST_LEN = 8192
PAGE_SIZE = 128

SB_WIDTH = D_MODEL // 2
SB_HEAD_DIM = 64
SB_HEADS = SB_WIDTH // SB_HEAD_DIM
SB_BLOCK = 128
SB_SCALE = SB_HEAD_DIM ** -0.5
SB_BIAS_INIT = -6.0
HG_WIDTH = D_MODEL - SB_WIDTH
HG_DK = 128
HG_DV = 128
HG_HEADS = HG_WIDTH // HG_DK
HG_CHUNK = 64
IN_SPLITS = (SB_WIDTH, SB_WIDTH, SB_WIDTH, HG_WIDTH, HG_WIDTH, HG_WIDTH, HG_WIDTH)
IN_WIDTH = sum(IN_SPLITS)
MIX_WIDTH = SB_WIDTH + HG_WIDTH
N_GROUPS = 4
EXPERTS_PER_GROUP = 8
TOP_K = 2
D_EXPERT = D_MODEL // 4
EPS = 1e-6
F32 = jnp.float32

kernel_name = 'hybrid_stickbreak_hgrn2_hmoe_step'


def rmsnorm(x, g):
    xf = x.astype(F32)
    y = xf * lax.rsqrt(jnp.mean(xf * xf, axis=-1, keepdims=True) + EPS)
    return (y * g.astype(F32)).astype(x.dtype)


def sb_weights(z, q_pos, k_pos):
    before = k_pos[None, :] < q_pos[:, None]
    log_keep = jnp.where(before, jax.nn.log_sigmoid(-z), 0.0)
    later = lax.cumsum(log_keep, axis=z.ndim - 1, reverse=True) - log_keep
    log_a = jnp.where(before, jax.nn.log_sigmoid(z) + later, -jnp.inf)
    return jnp.exp(log_a)


def sb_prompt(q, k, v, bias):
    B, T, H, D = q.shape
    nb = T // SB_BLOCK
    qb = q.astype(F32).reshape(B, nb, SB_BLOCK, H, D).swapaxes(0, 1)
    kf, vf = k.astype(F32), v.astype(F32)
    k_pos = jnp.arange(T)
    b = bias.astype(F32)[None, :, None, None]

    def one_block(args):
        q_blk, i = args
        q_pos = i * SB_BLOCK + jnp.arange(SB_BLOCK)
        z = jnp.einsum('bqhd,bshd->bhqs', q_blk, kf) * SB_SCALE + b
        a = sb_weights(z, q_pos, k_pos)
        return jnp.einsum('bhqs,bshd->bqhd', a, vf)

    o = lax.map(one_block, (qb, jnp.arange(nb)))
    return o.swapaxes(0, 1).reshape(B, T, H, D)


def sb_sample(q, k, v, bias, k_past, v_past):
    B, T, H, D = q.shape
    P = k_past.shape[1]
    qf = q.astype(F32)
    z = jnp.concatenate([
        jnp.einsum('bqhd,bshd->bhqs', qf, k_past.astype(F32)),
        jnp.einsum('bqhd,bshd->bhqs', qf, k.astype(F32))], axis=-1) * SB_SCALE
    z = z + bias.astype(F32)[None, :, None, None]
    a = sb_weights(z, P + jnp.arange(T), jnp.arange(P + T))
    return (jnp.einsum('bhqs,bshd->bqhd', a[..., :P], v_past.astype(F32))
            + jnp.einsum('bhqs,bshd->bqhd', a[..., P:], v.astype(F32)))


def hgrn2_scan(q, k, v, logf, s0):
    B, T, H, K = q.shape
    V = v.shape[-1]
    C = math.gcd(T, HG_CHUNK)
    n = T // C

    def chunks(a):
        return a.reshape(B, n, C, H, a.shape[-1]).transpose(1, 0, 3, 2, 4)

    incl = jnp.tril(jnp.ones((C, C), dtype=bool))[:, :, None]

    def step(S, inp):
        qc, kc, vc, gc = inp
        G = jnp.cumsum(gc, axis=2)
        o_inter = jnp.einsum('bhck,bhkv->bhcv', qc * jnp.exp(G), S)
        diff = G[:, :, :, None, :] - G[:, :, None, :, :]
        decay = jnp.exp(jnp.where(incl, diff, -jnp.inf))
        att = jnp.einsum('bhtk,bhsk,bhtsk->bhts', qc, kc, decay)
        o_intra = jnp.einsum('bhts,bhsv->bhtv', att, vc)
        G_last = G[:, :, -1:, :]
        S_new = (jnp.exp(G_last[:, :, 0, :])[..., None] * S
                 + jnp.einsum('bhsk,bhsv->bhkv', kc * jnp.exp(G_last - G), vc))
        return S_new, o_inter + o_intra

    S_fin, o = lax.scan(step, s0, (chunks(q), chunks(k), chunks(v), chunks(logf)))
    return o.transpose(1, 0, 3, 2, 4).reshape(B, T, H, V), S_fin


def hgrn2_mixer(q_b, f_b, i_b, g_b, lb, hg_norm, s0):
    B, T, _ = q_b.shape

    def heads(a, d):
        return a.astype(F32).reshape(B, T, HG_HEADS, d)

    fgate = lb + (1.0 - lb) * jax.nn.sigmoid(f_b.astype(F32))
    q = heads(jax.nn.silu(q_b.astype(F32)), HG_DK)
    k = heads(1.0 - fgate, HG_DK)
    logf = heads(jnp.log(fgate), HG_DK)
    v = heads(i_b, HG_DV)
    o, s_new = hgrn2_scan(q, k, v, logf, s0.astype(F32))
    o = rmsnorm(o, hg_norm) * jax.nn.silu(heads(g_b, HG_DV))
    return o.reshape(B, T, HG_WIDTH), s_new


def hier_moe(h, w_rg, b_rg, w_re, b_re, w_eg, w_eu, w_ed):
    N = h.shape[0]
    lg = jnp.matmul(h, w_rg).astype(F32) + b_rg.astype(F32)
    p_top, g_idx = lax.top_k(jax.nn.softmax(lg, axis=-1), 1)
    le = (jnp.matmul(h, w_re).astype(F32) + b_re.astype(F32)).reshape(N, N_GROUPS, EXPERTS_PER_GROUP)
    le_g = jnp.take_along_axis(le, g_idx[:, :, None], axis=1)[:, 0]
    top_v, top_i = lax.top_k(le_g, TOP_K)
    w_top = jax.nn.softmax(top_v, axis=-1) * p_top
    w_exp = jnp.einsum('nk,nke->ne', w_top, jax.nn.one_hot(top_i, EXPERTS_PER_GROUP, dtype=F32))
    gate = jax.nn.one_hot(g_idx[:, 0], N_GROUPS, dtype=F32)[:, :, None] * w_exp[:, None, :]
    y = jnp.zeros((N, h.shape[-1]), F32)
    for g in range(N_GROUPS):
        a = jnp.einsum('nd,edf->nef', h, w_eg[g]).astype(F32)
        u = jnp.einsum('nd,edf->nef', h, w_eu[g]).astype(F32)
        act = (jax.nn.silu(a) * u * gate[:, g, :, None]).astype(h.dtype)
        y = y + jnp.einsum('nef,efd->nd', act, w_ed[g]).astype(F32)
    return y.astype(h.dtype)


def decoder_layer(x, s0, sb_fn, norm_attn, w_in, sb_bias, sb_norm, lb, hg_norm, w_out, norm_ffn,
                  w_rg, b_rg, w_re, b_re, w_eg, w_eu, w_ed):
    B, T, D = x.shape
    h = rmsnorm(x, norm_attn)
    proj = jnp.einsum('btd,de->bte', h, w_in)
    offsets = [int(o) for o in np.cumsum(IN_SPLITS)[:-1]]
    q_a, k_a, v_a, q_b, f_b, i_b, g_b = jnp.split(proj, offsets, axis=-1)
    k_rows = k_a.reshape(B, T, SB_HEADS, SB_HEAD_DIM)
    v_rows = v_a.reshape(B, T, SB_HEADS, SB_HEAD_DIM)
    o_a = sb_fn(q_a.reshape(B, T, SB_HEADS, SB_HEAD_DIM), k_rows, v_rows, sb_bias).reshape(B, T, SB_WIDTH)
    o_a = rmsnorm(o_a, sb_norm)
    o_b, s_new = hgrn2_mixer(q_b, f_b, i_b, g_b, lb, hg_norm, s0)
    mix = jnp.concatenate([o_a, o_b], axis=-1).astype(x.dtype)
    x = x + jnp.einsum('bte,ed->btd', mix, w_out).astype(x.dtype)
    h2 = rmsnorm(x, norm_ffn).reshape(B * T, D)
    x = x + hier_moe(h2, w_rg, b_rg, w_re, b_re, w_eg, w_eu, w_ed).reshape(B, T, D).astype(x.dtype)
    return x, k_rows, v_rows, s_new


def setup_inputs(seed: int = 0) -> dict:
    key = jax.random.key(seed)
    ks = jax.random.split(key, 24)
    n_pages = PAST_LEN // PAGE_SIZE
    n_used = DEC_BATCH * n_pages
    n_pool = n_used + n_used // 4
    E = EXPERTS_PER_GROUP

    def nrm(k, shape, scale):
        return jax.random.normal(k, shape, F32) * scale

    def gain(k, shape):
        return 1.0 + nrm(k, shape, 0.05)

    page_table = jax.random.permutation(ks[5], n_pool)[:n_used].reshape(DEC_BATCH, n_pages).astype(jnp.int32)
    return {
        'x_prompt': nrm(ks[0], (BATCH, SEQ, D_MODEL), 1.0),
        'x_sample': nrm(ks[1], (DEC_BATCH, DEC_SEQ, D_MODEL), 1.0),
        'cache_k': nrm(ks[2], (DEPTH, n_pool, PAGE_SIZE, SB_HEADS, SB_HEAD_DIM), 1.0),
        'cache_v': nrm(ks[3], (DEPTH, n_pool, PAGE_SIZE, SB_HEADS, SB_HEAD_DIM), 1.0),
        'state_hgrn': nrm(ks[4], (DEPTH, DEC_BATCH, HG_HEADS, HG_DK, HG_DV), 0.5),
        'page_table': page_table,
        'norm_attn': gain(ks[6], (DEPTH, D_MODEL)),
        'w_in': nrm(ks[7], (DEPTH, D_MODEL, IN_WIDTH), D_MODEL ** -0.5),
        'sb_logit_bias': SB_BIAS_INIT + nrm(ks[21], (DEPTH, SB_HEADS), 0.5),
        'sb_norm': gain(ks[8], (DEPTH, SB_WIDTH)),
        'hg_lb_logits': nrm(ks[9], (DEPTH + 1, HG_WIDTH), 0.5),
        'hg_norm': gain(ks[10], (DEPTH, HG_DV)),
        'w_out': nrm(ks[11], (DEPTH, MIX_WIDTH, D_MODEL), MIX_WIDTH ** -0.5),
        'norm_ffn': gain(ks[12], (DEPTH, D_MODEL)),
        'w_router_group': nrm(ks[13], (DEPTH, D_MODEL, N_GROUPS), D_MODEL ** -0.5),
        'b_router_group': nrm(ks[14], (DEPTH, N_GROUPS), 0.01),
        'w_router_expert': nrm(ks[15], (DEPTH, D_MODEL, N_GROUPS * E), D_MODEL ** -0.5),
        'b_router_expert': nrm(ks[16], (DEPTH, N_GROUPS * E), 0.01),
        'w_exp_gate': nrm(ks[17], (DEPTH, N_GROUPS, E, D_MODEL, D_EXPERT), D_MODEL ** -0.5),
        'w_exp_up': nrm(ks[18], (DEPTH, N_GROUPS, E, D_MODEL, D_EXPERT), D_MODEL ** -0.5),
        'w_exp_down': nrm(ks[19], (DEPTH, N_GROUPS, E, D_EXPERT, D_MODEL), D_EXPERT ** -0.5),
        'norm_final': gain(ks[20], (D_MODEL,)),
    }


def reference(x_prompt, x_sample, cache_k, cache_v, state_hgrn, page_table, norm_attn, w_in,
              sb_logit_bias, sb_norm, hg_lb_logits, hg_norm, w_out, norm_ffn, w_router_group,
              b_router_group, w_router_expert, b_router_expert, w_exp_gate, w_exp_up, w_exp_down,
              norm_final):
    lb_all = jnp.cumsum(jax.nn.softmax(hg_lb_logits.astype(F32), axis=0), axis=0)
    bp = x_prompt.shape[0]
    db = x_sample.shape[0]
    xp, xs = x_prompt, x_sample
    kp, vp, sp, ksm, vsm, ssm = [], [], [], [], [], []
    for l in range(DEPTH):
        w_l = (norm_attn[l], w_in[l], sb_logit_bias[l], sb_norm[l], lb_all[l], hg_norm[l], w_out[l],
               norm_ffn[l], w_router_group[l], b_router_group[l], w_router_expert[l],
               b_router_expert[l], w_exp_gate[l], w_exp_up[l], w_exp_down[l])
        s0 = jnp.zeros((bp, HG_HEADS, HG_DK, HG_DV), F32)
        xp, k_new, v_new, s_new = decoder_layer(xp, s0, sb_prompt, *w_l)
        kp.append(k_new)
        vp.append(v_new)
        sp.append(s_new.astype(state_hgrn.dtype))
        k_past = cache_k[l][page_table].reshape(db, -1, SB_HEADS, SB_HEAD_DIM)
        v_past = cache_v[l][page_table].reshape(db, -1, SB_HEADS, SB_HEAD_DIM)
        sb_fn = functools.partial(sb_sample, k_past=k_past, v_past=v_past)
        xs, k_new, v_new, s_new = decoder_layer(xs, state_hgrn[l], sb_fn, *w_l)
        ksm.append(k_new)
        vsm.append(v_new)
        ssm.append(s_new.astype(state_hgrn.dtype))
    y_prompt = rmsnorm(xp, norm_final)
    y_sample = rmsnorm(xs, norm_final)
    return (y_prompt, y_sample, jnp.stack(kp), jnp.stack(vp), jnp.stack(sp),
            jnp.stack(ksm), jnp.stack(vsm), jnp.stack(ssm))
```

```python
import functools

import jax
import jax.numpy as jnp
from jax import lax
from jax.experimental import pallas as pl
from jax.experimental.pallas import tpu as pltpu

F32 = jnp.float32
BF16 = jnp.bfloat16
EPS = 1e-6

SB_HEADS = 8
SB_HEAD_DIM = 64
SB_WIDTH = SB_HEADS * SB_HEAD_DIM
SB_SCALE = SB_HEAD_DIM ** -0.5
HG_HEADS = 4
HG_DK = 128
HG_DV = 128
HG_WIDTH = HG_HEADS * HG_DK
HG_CHUNK = 64
HG_SUB = 16
N_GROUPS = 4
EXPERTS_PER_GROUP = 8
N_EXPERTS = N_GROUPS * EXPERTS_PER_GROUP
ROUTER_LANES = 128
PAGE_SIZE = 128

VMEM_LIMIT_BYTES = 56 * 1024 * 1024

NT_DIMS = (((1,), (1,)), ((), ()))
TN_DIMS = (((0,), (0,)), ((), ()))


def _cparams(*sem):
    return pltpu.CompilerParams(dimension_semantics=sem, vmem_limit_bytes=VMEM_LIMIT_BYTES)


def _split2(x):
    hi = x.astype(BF16)
    lo = (x - hi.astype(F32)).astype(BF16)
    return hi, lo


def _split3(x):
    hi = x.astype(BF16)
    r = x - hi.astype(F32)
    mid = r.astype(BF16)
    lo = (r - mid.astype(F32)).astype(BF16)
    return hi, mid, lo


def _softplus(z):
    return jnp.maximum(z, 0.0) + jnp.log(1.0 + jnp.exp(-jnp.abs(z)))


def _sigmoid(x):
    return 1.0 / (1.0 + jnp.exp(-x))


def _inproj_kernel(x_ref, g_ref, w_ref, q_ref, k_ref, v_ref, hg_ref):
    x = x_ref[...]
    ms = jnp.mean(x * x, axis=-1, keepdims=True)
    h = (x * lax.rsqrt(ms + EPS) * g_ref[...]).astype(BF16)
    w = SB_WIDTH
    q_ref[...] = jnp.dot(h, w_ref[:, 0:w], preferred_element_type=F32)
    k_ref[...] = jnp.dot(h, w_ref[:, w:2 * w], preferred_element_type=F32)
    v_ref[...] = jnp.dot(h, w_ref[:, 2 * w:3 * w], preferred_element_type=F32)
    hg_ref[...] = jnp.dot(h, w_ref[:, 3 * w:], preferred_element_type=F32)


def _inproj(x2d, gain, w_in_bf16, tm):
    n, d = x2d.shape
    e = w_in_bf16.shape[1]
    hgw = e - 3 * SB_WIDTH
    return pl.pallas_call(
        _inproj_kernel,
        grid=(n // tm,),
        in_specs=[
            pl.BlockSpec((tm, d), lambda i: (i, 0)),
            pl.BlockSpec((1, d), lambda i: (0, 0)),
            pl.BlockSpec((d, e), lambda i: (0, 0)),
        ],
        out_specs=[
            pl.BlockSpec((tm, SB_WIDTH), lambda i: (i, 0)),
            pl.BlockSpec((tm, SB_WIDTH), lambda i: (i, 0)),
            pl.BlockSpec((tm, SB_WIDTH), lambda i: (i, 0)),
            pl.BlockSpec((tm, hgw), lambda i: (i, 0)),
        ],
        out_shape=[
            jax.ShapeDtypeStruct((n, SB_WIDTH), F32),
            jax.ShapeDtypeStruct((n, SB_WIDTH), F32),
            jax.ShapeDtypeStruct((n, SB_WIDTH), F32),
            jax.ShapeDtypeStruct((n, hgw), F32),
        ],
        compiler_params=_cparams("parallel"),
        name="inproj",
    )(x2d, gain.reshape(1, d), w_in_bf16)


def _sb_block(z, tri, carry, before):
    lk = -_softplus(z)
    if before is not None:
        lk = jnp.where(before, lk, 0.0)
    hi, lo = _split2(lk)
    cs = (jnp.dot(hi, tri, preferred_element_type=F32)
          + jnp.dot(lo, tri, preferred_element_type=F32))
    la = z + cs + carry
    if before is not None:
        la = jnp.where(before, la, -jnp.inf)
    return jnp.exp(la), carry + cs[:, 0:1]


def _tri_incl(n):
    j = lax.broadcasted_iota(jnp.int32, (n, n), 0)
    s = lax.broadcasted_iota(jnp.int32, (n, n), 1)
    return (j >= s).astype(BF16)


def _sb_prompt_kernel(bias_ref, q_ref, k_ref, v_ref, o_ref, *, tq):
    h = pl.program_id(1)
    i = pl.program_id(2)
    bias = bias_ref[h]
    q = q_ref[0, 0]
    tri = _tri_incl(tq)
    row = lax.broadcasted_iota(jnp.int32, (tq, tq), 0)
    col = lax.broadcasted_iota(jnp.int32, (tq, tq), 1)

    def step(j, carry, acc, before):
        start = pl.multiple_of(j * tq, tq)
        kj = k_ref[0, 0, pl.ds(start, tq), :]
        vj = v_ref[0, 0, pl.ds(start, tq), :]
        z = lax.dot_general(q, kj, NT_DIMS, preferred_element_type=F32) + bias
        a, carry = _sb_block(z, tri, carry, before)
        acc = acc + jnp.dot(a.astype(BF16), vj, preferred_element_type=F32)
        return carry, acc

    carry0 = jnp.zeros((tq, 1), F32)
    acc0 = jnp.zeros((tq, SB_HEAD_DIM), F32)
    carry, acc = step(i, carry0, acc0, col < row)

    def body(n, c):
        return step(i - 1 - n, c[0], c[1], None)

    carry, acc = lax.fori_loop(0, i, body, (carry, acc))
    o_ref[0, 0] = acc


def _sb_prompt(q, k, v, bias, tq):
    b, h, t, d = q.shape
    return pl.pallas_call(
        functools.partial(_sb_prompt_kernel, tq=tq),
        grid=(b, h, t // tq),
        in_specs=[
            pl.BlockSpec(memory_space=pltpu.SMEM),
            pl.BlockSpec((1, 1, tq, d), lambda bi, hi, i: (bi, hi, i, 0)),
            pl.BlockSpec((1, 1, t, d), lambda bi, hi, i: (bi, hi, 0, 0)),
            pl.BlockSpec((1, 1, t, d), lambda bi, hi, i: (bi, hi, 0, 0)),
        ],
        out_specs=pl.BlockSpec((1, 1, tq, d), lambda bi, hi, i: (bi, hi, i, 0)),
        out_shape=jax.ShapeDtypeStruct((b, h, t, d), F32),
        compiler_params=_cparams("parallel", "parallel", "arbitrary"),
        name="sb_prompt",
    )(bias, q, k, v)


def _sb_sample_kernel(pt_ref, qbd_ref, bias_ref, kn_ref, vn_ref, *rest, pages_per_step, t_new):
    g = pages_per_step
    k_refs = rest[:g]
    v_refs = rest[g:2 * g]
    o_ref = rest[2 * g]
    carry_ref, acc_ref, kpad_ref, vpad_ref = rest[2 * g + 1:]
    step_id = pl.program_id(1)
    rows = qbd_ref.shape[1]
    qbd = qbd_ref[0]
    bias = bias_ref[...]
    tri = _tri_incl(PAGE_SIZE)

    def block(kpage, vpage, carry, before):
        z = lax.dot_general(qbd, kpage.astype(BF16), NT_DIMS, preferred_element_type=F32) + bias
        a, carry = _sb_block(z, tri, carry, before)
        return carry, jnp.dot(a.astype(BF16), vpage.astype(BF16), preferred_element_type=F32)

    @pl.when(step_id == 0)
    def _():
        kpad_ref[...] = jnp.zeros_like(kpad_ref)
        vpad_ref[...] = jnp.zeros_like(vpad_ref)
        kpad_ref[0:kn_ref.shape[1], :] = kn_ref[0]
        vpad_ref[0:vn_ref.shape[1], :] = vn_ref[0]
        qi = lax.broadcasted_iota(jnp.int32, (rows, PAGE_SIZE), 0) % t_new
        s = lax.broadcasted_iota(jnp.int32, (rows, PAGE_SIZE), 1)
        carry, contrib = block(kpad_ref[...], vpad_ref[...], jnp.zeros((rows, 1), F32), s < qi)
        carry_ref[...] = jnp.broadcast_to(carry, carry_ref.shape)
        acc_ref[...] = contrib

    carry = carry_ref[:, 0:1]
    acc = acc_ref[...]
    for p in range(g):
        carry, contrib = block(k_refs[p][0], v_refs[p][0], carry, None)
        acc = acc + contrib
    carry_ref[...] = jnp.broadcast_to(carry, carry_ref.shape)
    acc_ref[...] = acc

    @pl.when(step_id == pl.num_programs(1) - 1)
    def _():
        o_ref[0] = acc


def _sb_sample(qbd, bias_rows, k_new, v_new, cache_k, cache_v, page_table_flat, n_pages, t_new,
               pages_per_step):
    b, rows, w = qbd.shape
    g = pages_per_step
    steps = n_pages // g

    def page_map(p):
        return lambda bi, si, pt: (pt[bi * n_pages + (n_pages - 1 - (si * g + p))], 0, 0)

    page_specs = [pl.BlockSpec((1, PAGE_SIZE, w), page_map(p)) for p in range(g)]
    grid_spec = pltpu.PrefetchScalarGridSpec(
        num_scalar_prefetch=1,
        grid=(b, steps),
        in_specs=[
            pl.BlockSpec((1, rows, w), lambda bi, si, pt: (bi, 0, 0)),
            pl.BlockSpec((rows, PAGE_SIZE), lambda bi, si, pt: (0, 0)),
            pl.BlockSpec((1, k_new.shape[1], w), lambda bi, si, pt: (bi, 0, 0)),
            pl.BlockSpec((1, v_new.shape[1], w), lambda bi, si, pt: (bi, 0, 0)),
        ] + page_specs + page_specs,
        out_specs=pl.BlockSpec((1, rows, w), lambda bi, si, pt: (bi, 0, 0)),
        scratch_shapes=[
            pltpu.VMEM((rows, PAGE_SIZE), F32),
            pltpu.VMEM((rows, w), F32),
            pltpu.VMEM((PAGE_SIZE, w), F32),
            pltpu.VMEM((PAGE_SIZE, w), F32),
        ],
    )
    return pl.pallas_call(
        functools.partial(_sb_sample_kernel, pages_per_step=g, t_new=t_new),
        grid_spec=grid_spec,
        out_shape=jax.ShapeDtypeStruct((b, rows, w), F32),
        compiler_params=_cparams("parallel", "arbitrary"),
        name="sb_sample",
    )(page_table_flat, qbd, bias_rows, k_new, v_new, *([cache_k] * g), *([cache_v] * g))


def _lower_bound(lb_logits, layer):
    m = jnp.max(lb_logits, axis=0, keepdims=True)
    e = jnp.exp(lb_logits - m)
    return jnp.sum(e[0:layer + 1], axis=0, keepdims=True) / jnp.sum(e, axis=0, keepdims=True)


def _hg_out(o, gain, gb):
    ms = jnp.mean(o * o, axis=-1, keepdims=True)
    return (o * lax.rsqrt(ms + EPS) * gain) * (gb * _sigmoid(gb))


def _hgrn_prompt_kernel(lbl_ref, gain_ref, qb_ref, fb_ref, ib_ref, gb_ref, o_ref, s_ref, st_ref,
                        *, layer, n_chunks):
    c = HG_CHUNK
    t_id = pl.program_id(2)

    @pl.when(t_id == 0)
    def _():
        st_ref[...] = jnp.zeros_like(st_ref)

    lb = _lower_bound(lbl_ref[...], layer)
    gain = gain_ref[...]
    ti = lax.broadcasted_iota(jnp.int32, (c, c), 0)
    si = lax.broadcasted_iota(jnp.int32, (c, c), 1)
    ltri = (ti >= si).astype(BF16)
    rowc = lax.broadcasted_iota(jnp.int32, (c, HG_DK), 0)
    rows = lax.broadcasted_iota(jnp.int32, (HG_SUB, HG_DK), 0)

    def chunk(n, _):
        sl = pl.ds(pl.multiple_of(n * c, c), c)
        f = lb + (1.0 - lb) * _sigmoid(fb_ref[0, sl, :])
        logf = jnp.log(f)
        kk = 1.0 - f
        qx = qb_ref[0, sl, :]
        qq = qx * _sigmoid(qx)
        v = ib_ref[0, sl, :]
        vb = v.astype(BF16)
        p1, p2, p3 = _split3(logf)
        gcum = (jnp.dot(ltri, p1, preferred_element_type=F32)
                + jnp.dot(ltri, p2, preferred_element_type=F32)
                + jnp.dot(ltri, p3, preferred_element_type=F32))
        g_last = gcum[c - 1:c, :]
        st = st_ref[...]
        o = lax.dot_general((qq * jnp.exp(gcum)).astype(BF16), st.astype(BF16), NT_DIMS,
                            preferred_element_type=F32)
        parts = [jnp.zeros((HG_SUB, HG_DV), F32)]
        for i in range(1, c // HG_SUB):
            lo = i * HG_SUB
            r = gcum[lo - 1:lo, :]
            qh = qq[lo:lo + HG_SUB] * jnp.exp(gcum[lo:lo + HG_SUB] - r)
            kh = kk * jnp.exp(jnp.where(rowc < lo, r - gcum, -jnp.inf))
            att = lax.dot_general(qh.astype(BF16), kh.astype(BF16), NT_DIMS,
                                  preferred_element_type=F32)
            parts.append(jnp.dot(att.astype(BF16), vb, preferred_element_type=F32))
        diag = []
        for i in range(c // HG_SUB):
            lo = i * HG_SUB
            gs = gcum[lo:lo + HG_SUB]
            qs = qq[lo:lo + HG_SUB]
            ks = kk[lo:lo + HG_SUB]
            vs = v[lo:lo + HG_SUB]
            od = jnp.zeros((HG_SUB, HG_DV), F32)
            for s in range(HG_SUB):
                e = jnp.exp(jnp.where(rows >= s, gs - gs[s:s + 1, :], -jnp.inf))
                a = jnp.sum(qs * (ks[s:s + 1, :] * e), axis=1, keepdims=True)
                od = od + a * vs[s:s + 1, :]
            diag.append(od + parts[i])
        o = o + jnp.concatenate(diag, axis=0)
        o_ref[0, sl, :] = _hg_out(o, gain, gb_ref[0, sl, :])
        ke = kk * jnp.exp(g_last - gcum)
        st_ref[...] = st * jnp.exp(g_last) + lax.dot_general(
            vb, ke.astype(BF16), TN_DIMS, preferred_element_type=F32)
        return 0

    lax.fori_loop(0, n_chunks, chunk, 0)

    @pl.when(t_id == pl.num_programs(2) - 1)
    def _():
        s_ref[0, 0] = st_ref[...].T


def _hgrn_prompt(hg, lb_logits, gain, layer, tt):
    b, t, _ = hg.shape
    nh = HG_HEADS

    def col(which):
        return pl.BlockSpec((1, tt, HG_DK), lambda bi, hi, ti: (bi, ti, which * nh + hi))

    return pl.pallas_call(
        functools.partial(_hgrn_prompt_kernel, layer=layer, n_chunks=tt // HG_CHUNK),
        grid=(b, nh, t // tt),
        in_specs=[
            pl.BlockSpec((lb_logits.shape[0], HG_DK), lambda bi, hi, ti: (0, hi)),
            pl.BlockSpec((1, HG_DV), lambda bi, hi, ti: (0, 0)),
            col(0), col(1), col(2), col(3),
        ],
        out_specs=[
            pl.BlockSpec((1, tt, HG_DV), lambda bi, hi, ti: (bi, ti, hi)),
            pl.BlockSpec((1, 1, HG_DK, HG_DV), lambda bi, hi, ti: (bi, hi, 0, 0)),
        ],
        out_shape=[
            jax.ShapeDtypeStruct((b, t, HG_WIDTH), F32),
            jax.ShapeDtypeStruct((b, nh, HG_DK, HG_DV), F32),
        ],
        scratch_shapes=[pltpu.VMEM((HG_DV, HG_DK), F32)],
        compiler_params=_cparams("parallel", "parallel", "arbitrary"),
        name="hgrn_prompt",
    )(lb_logits, gain.reshape(1, HG_DV), hg, hg, hg, hg)


def _hgrn_sample_kernel(lbl_ref, gain_ref, hg_ref, s0_ref, o_ref, s_ref, *, layer, t_new):
    w = HG_WIDTH
    lb_all = _lower_bound(lbl_ref[...], layer)
    gain = gain_ref[...]
    x = hg_ref[0]
    outs = []
    for h in range(HG_HEADS):
        c0 = h * HG_DK
        lb = lb_all[:, c0:c0 + HG_DK]
        qx = x[:, c0:c0 + HG_DK]
        f = lb + (1.0 - lb) * _sigmoid(x[:, w + c0:w + c0 + HG_DK])
        v = x[:, 2 * w + c0:2 * w + c0 + HG_DV]
        gb = x[:, 3 * w + c0:3 * w + c0 + HG_DV]
        stack = jnp.concatenate(
            [f, 1.0 - f, qx * _sigmoid(qx),
             jnp.zeros((HG_DK - 3 * t_new, HG_DK), F32)], axis=0)
        cols = stack.T
        s = s0_ref[0, h]
        o_rows = []
        for t in range(t_new):
            fc = cols[:, t:t + 1]
            kc = cols[:, t_new + t:t_new + t + 1]
            qc = cols[:, 2 * t_new + t:2 * t_new + t + 1]
            s = fc * s + kc * v[t:t + 1, :]
            o_rows.append(jnp.sum(qc * s, axis=0, keepdims=True))
        s_ref[0, h] = s
        outs.append(_hg_out(jnp.concatenate(o_rows, axis=0), gain, gb))
    o_ref[0] = jnp.concatenate(outs, axis=1)


def _hgrn_sample(hg, state, lb_logits, gain, layer):
    b, t_new, e = hg.shape
    nh = HG_HEADS
    return pl.pallas_call(
        functools.partial(_hgrn_sample_kernel, layer=layer, t_new=t_new),
        grid=(b,),
        in_specs=[
            pl.BlockSpec(lb_logits.shape, lambda bi: (0, 0)),
            pl.BlockSpec((1, HG_DV), lambda bi: (0, 0)),
            pl.BlockSpec((1, t_new, e), lambda bi: (bi, 0, 0)),
            pl.BlockSpec((1, nh, HG_DK, HG_DV), lambda bi: (bi, 0, 0, 0)),
        ],
        out_specs=[
            pl.BlockSpec((1, t_new, HG_WIDTH), lambda bi: (bi, 0, 0)),
            pl.BlockSpec((1, nh, HG_DK, HG_DV), lambda bi: (bi, 0, 0, 0)),
        ],
        out_shape=[
            jax.ShapeDtypeStruct((b, t_new, HG_WIDTH), F32),
            jax.ShapeDtypeStruct((b, nh, HG_DK, HG_DV), F32),
        ],
        compiler_params=_cparams("parallel"),
        name="hgrn_sample",
    )(lb_logits, gain.reshape(1, HG_DV), hg, state)


def _route(logits):
    lane = lax.broadcasted_iota(jnp.int32, logits.shape, 1)
    neg = -jnp.inf
    big = ROUTER_LANES

    def top(mask):
        val = jnp.max(jnp.where(mask, logits, neg), axis=1, keepdims=True)
        idx = jnp.min(jnp.where(mask & (logits == val), lane, big), axis=1, keepdims=True)
        return val, idx

    is_group = lane < N_GROUPS
    gmax, gidx = top(is_group)
    p_top = 1.0 / jnp.sum(jnp.where(is_group, jnp.exp(logits - gmax), 0.0), axis=1, keepdims=True)
    lo = N_GROUPS + gidx * EXPERTS_PER_GROUP
    in_group = (lane >= lo) & (lane < lo + EXPERTS_PER_GROUP)
    v1, i1 = top(in_group)
    v2, i2 = top(in_group & (lane != i1))
    e2 = jnp.exp(v2 - v1)
    w1 = p_top / (1.0 + e2)
    w2 = p_top * e2 / (1.0 + e2)
    return jnp.where(lane == i1, w1, 0.0) + jnp.where(lane == i2, w2, 0.0)


def _post_kernel(x_ref, oa_ref, ob_ref, sbg_ref, wo_ref, ng_ref, wr_ref, br_ref,
                 x1_ref, h2_ref, gate_ref):
    oa = oa_ref[...]
    ms = jnp.mean(oa * oa, axis=-1, keepdims=True)
    oa = oa * lax.rsqrt(ms + EPS) * sbg_ref[...]
    w = SB_WIDTH
    mixed = (jnp.dot(oa.astype(BF16), wo_ref[0:w, :], preferred_element_type=F32)
             + jnp.dot(ob_ref[...].astype(BF16), wo_ref[w:, :], preferred_element_type=F32))
    x1 = x_ref[...] + mixed
    x1_ref[...] = x1
    ms = jnp.mean(x1 * x1, axis=-1, keepdims=True)
    h2 = x1 * lax.rsqrt(ms + EPS) * ng_ref[...]
    h2_ref[...] = h2.astype(BF16)
    hh, hl = _split2(h2)
    logits = (jnp.dot(hh, wr_ref[0], preferred_element_type=F32)
              + jnp.dot(hl, wr_ref[0], preferred_element_type=F32)
              + jnp.dot(hh, wr_ref[1], preferred_element_type=F32)) + br_ref[...]
    gate_ref[...] = _route(logits)


def _post(x2d, oa, ob, sb_gain, w_out_bf16, ffn_gain, w_router_hl, b_router, tm):
    n, d = x2d.shape
    row = lambda i: (i, 0)
    fixed = lambda i: (0, 0)
    return pl.pallas_call(
        _post_kernel,
        grid=(n // tm,),
        in_specs=[
            pl.BlockSpec((tm, d), row),
            pl.BlockSpec((tm, SB_WIDTH), row),
            pl.BlockSpec((tm, HG_WIDTH), row),
            pl.BlockSpec((1, SB_WIDTH), fixed),
            pl.BlockSpec(w_out_bf16.shape, fixed),
            pl.BlockSpec((1, d), fixed),
            pl.BlockSpec(w_router_hl.shape, lambda i: (0, 0, 0)),
            pl.BlockSpec((1, ROUTER_LANES), fixed),
        ],
        out_specs=[
            pl.BlockSpec((tm, d), row),
            pl.BlockSpec((tm, d), row),
            pl.BlockSpec((tm, ROUTER_LANES), row),
        ],
        out_shape=[
            jax.ShapeDtypeStruct((n, d), F32),
            jax.ShapeDtypeStruct((n, d), BF16),
            jax.ShapeDtypeStruct((n, ROUTER_LANES), F32),
        ],
        compiler_params=_cparams("parallel"),
        name="post",
    )(x2d, oa, ob, sb_gain.reshape(1, -1), w_out_bf16, ffn_gain.reshape(1, d), w_router_hl,
      b_router)


def _moe_kernel(x1_ref, h2_ref, gate_ref, wg_ref, wu_ref, wd_ref, fg_ref, y_ref, acc_ref):
    e = pl.program_id(1)

    @pl.when(e == 0)
    def _():
        acc_ref[...] = jnp.zeros_like(acc_ref)

    h2 = h2_ref[...]
    gate = gate_ref[...]
    lane = lax.broadcasted_iota(jnp.int32, gate.shape, 1)
    gcol = jnp.sum(jnp.where(lane == N_GROUPS + e, gate, 0.0), axis=1, keepdims=True)
    a = jnp.dot(h2, wg_ref[0], preferred_element_type=F32)
    u = jnp.dot(h2, wu_ref[0], preferred_element_type=F32)
    act = (a * _sigmoid(a)) * u * gcol
    acc_ref[...] += jnp.dot(act.astype(BF16), wd_ref[0], preferred_element_type=F32)

    @pl.when(e == pl.num_programs(1) - 1)
    def _():
        x2 = x1_ref[...] + acc_ref[...]
        ms = jnp.mean(x2 * x2, axis=-1, keepdims=True)
        y_ref[...] = x2 * lax.rsqrt(ms + EPS) * fg_ref[...]


def _moe(x1, h2, gate, wg, wu, wd, final_gain, tm):
    n, d = x1.shape
    ne, _, df = wg.shape
    row = lambda i, e: (i, 0)
    return pl.pallas_call(
        _moe_kernel,
        grid=(n // tm, ne),
        in_specs=[
            pl.BlockSpec((tm, d), row),
            pl.BlockSpec((tm, d), row),
            pl.BlockSpec((tm, ROUTER_LANES), row),
            pl.BlockSpec((1, d, df), lambda i, e: (e, 0, 0)),
            pl.BlockSpec((1, d, df), lambda i, e: (e, 0, 0)),
            pl.BlockSpec((1, df, d), lambda i, e: (e, 0, 0)),
            pl.BlockSpec((1, d), lambda i, e: (0, 0)),
        ],
        out_specs=pl.BlockSpec((tm, d), row),
        out_shape=jax.ShapeDtypeStruct((n, d), F32),
        scratch_shapes=[pltpu.VMEM((tm, d), F32)],
        compiler_params=_cparams("parallel", "arbitrary"),
        name="moe",
    )(x1, h2, gate, wg, wu, wd, final_gain.reshape(1, d))


def _tile(n, pref):
    return pref if n % pref == 0 else n


def _router_params(w_rg, b_rg, w_re, b_re):
    d = w_rg.shape[0]
    pad = ROUTER_LANES - N_GROUPS - N_EXPERTS
    w = jnp.concatenate([w_rg, w_re, jnp.zeros((d, pad), F32)], axis=1)
    hi = w.astype(BF16)
    lo = (w - hi.astype(F32)).astype(BF16)
    b = jnp.concatenate([b_rg, b_re, jnp.zeros((pad,), F32)]).reshape(1, ROUTER_LANES)
    return jnp.stack([hi, lo]), b


def _ffn(x2d, oa, ob, sb_gain, w_out_bf16, ffn_gain, w_router_hl, b_router, wg, wu, wd, final_gain):
    n = x2d.shape[0]
    x1, h2, gate = _post(x2d, oa, ob, sb_gain, w_out_bf16, ffn_gain, w_router_hl, b_router,
                         _tile(n, 512))
    return _moe(x1, h2, gate, wg, wu, wd, final_gain, _tile(n, 1024))


def kernel(x_prompt, x_sample, cache_k, cache_v, state_hgrn, page_table, norm_attn, w_in,
           sb_logit_bias, sb_norm, hg_lb_logits, hg_norm, w_out, norm_ffn, w_router_group,
           b_router_group, w_router_expert, b_router_expert, w_exp_gate, w_exp_up, w_exp_down,
           norm_final):
    depth = w_in.shape[0]
    assert depth == 1, "single-layer trunk"
    l = 0
    bp, t, d = x_prompt.shape
    db, tn, _ = x_sample.shape
    nh, hd = SB_HEADS, SB_HEAD_DIM
    n_pages = page_table.shape[1]

    w_in_b = w_in[l].astype(BF16)
    w_out_b = w_out[l].astype(BF16)
    wr_hl, b_r = _router_params(w_router_group[l], b_router_group[l], w_router_expert[l],
                                b_router_expert[l])
    df = w_exp_gate.shape[-1]
    wg = w_exp_gate[l].reshape(N_EXPERTS, d, df).astype(BF16)
    wu = w_exp_up[l].reshape(N_EXPERTS, d, df).astype(BF16)
    wd = w_exp_down[l].reshape(N_EXPERTS, df, d).astype(BF16)
    bias = sb_logit_bias[l].astype(F32)

    xp = x_prompt.reshape(bp * t, d)
    qp, kp, vp, hgp = _inproj(xp, norm_attn[l], w_in_b, 256)

    def heads(a, scale=None):
        a = a.reshape(bp, t, nh, hd)
        if scale is not None:
            a = a * scale
        return a.astype(BF16).transpose(0, 2, 1, 3)

    oap = _sb_prompt(heads(qp, SB_SCALE), heads(kp), heads(vp), bias, 256)
    oap = oap.transpose(0, 2, 1, 3).reshape(bp * t, SB_WIDTH)
    obp, sp = _hgrn_prompt(hgp.reshape(bp, t, -1), hg_lb_logits, hg_norm[l], l, 512)
    yp = _ffn(xp, oap, obp.reshape(bp * t, HG_WIDTH), sb_norm[l], w_out_b, norm_ffn[l], wr_hl, b_r,
              wg, wu, wd, norm_final)

    xs = x_sample.reshape(db * tn, d)
    qs, ks, vs, hgs = _inproj(xs, norm_attn[l], w_in_b, 256)
    head_of_lane = jnp.arange(SB_WIDTH) // hd
    onehot = (head_of_lane[None, :] == jnp.arange(nh)[:, None]).astype(F32)
    qbd = (qs.reshape(db, 1, tn, SB_WIDTH) * SB_SCALE) * onehot[None, :, None, :]
    qbd = qbd.reshape(db, nh * tn, SB_WIDTH).astype(BF16)
    bias_rows = jnp.broadcast_to(jnp.repeat(bias, tn)[:, None], (nh * tn, PAGE_SIZE))
    pad = ((0, 0), (0, 8 - tn), (0, 0))
    kn = jnp.pad(ks.reshape(db, tn, SB_WIDTH), pad)
    vn = jnp.pad(vs.reshape(db, tn, SB_WIDTH), pad)
    ck = cache_k[l].reshape(-1, PAGE_SIZE, SB_WIDTH)
    cv = cache_v[l].reshape(-1, PAGE_SIZE, SB_WIDTH)
    o_full = _sb_sample(qbd, bias_rows, kn, vn, ck, cv, page_table.reshape(-1), n_pages, tn, 8)
    oas = jnp.sum(o_full.reshape(db, nh, tn, SB_WIDTH) * onehot[None, :, None, :], axis=1)
    obs, ss = _hgrn_sample(hgs.reshape(db, tn, -1), state_hgrn[l], hg_lb_logits, hg_norm[l], l)
    ys = _ffn(xs, oas.reshape(db * tn, SB_WIDTH), obs.reshape(db * tn, HG_WIDTH), sb_norm[l],
              w_out_b, norm_ffn[l], wr_hl, b_r, wg, wu, wd, norm_final)

    sdt = state_hgrn.dtype
    return (yp.reshape(bp, t, d), ys.reshape(db, tn, d),
            kp.reshape(1, bp, t, nh, hd), vp.reshape(1, bp, t, nh, hd), sp.astype(sdt)[None],
            ks.reshape(1, db, tn, nh, hd), vs.reshape(1, db, tn, nh, hd), ss.astype(sdt)[None])
```

```python
import functools

import jax
import jax.numpy as jnp
from jax import lax
from jax.experimental import pallas as pl
from jax.experimental.pallas import tpu as pltpu

F32 = jnp.float32
BF16 = jnp.bfloat16
EPS = 1e-6

SB_HEADS = 8
SB_HEAD_DIM = 64
SB_WIDTH = SB_HEADS * SB_HEAD_DIM
SB_SCALE = SB_HEAD_DIM ** -0.5
HG_HEADS = 4
HG_DK = 128
HG_DV = 128
HG_WIDTH = HG_HEADS * HG_DK
HG_CHUNK = 64
HG_SUB = 16
N_GROUPS = 4
EXPERTS_PER_GROUP = 8
N_EXPERTS = N_GROUPS * EXPERTS_PER_GROUP
ROUTER_LANES = 128
PAGE_SIZE = 128
SB_KEY_BLOCK = 128
SB_QUERY_BLOCK = 512

VMEM_LIMIT_BYTES = 56 * 1024 * 1024

NT_DIMS = (((1,), (1,)), ((), ()))
TN_DIMS = (((0,), (0,)), ((), ()))


def _cparams(*sem):
    return pltpu.CompilerParams(dimension_semantics=sem, vmem_limit_bytes=VMEM_LIMIT_BYTES)


def _split2(x):
    hi = x.astype(BF16)
    lo = (x - hi.astype(F32)).astype(BF16)
    return hi, lo


def _split3(x):
    hi = x.astype(BF16)
    r = x - hi.astype(F32)
    mid = r.astype(BF16)
    lo = (r - mid.astype(F32)).astype(BF16)
    return hi, mid, lo


def _softplus(z):
    return jnp.maximum(z, 0.0) + jnp.log(1.0 + jnp.exp(-jnp.abs(z)))


def _sigmoid(x):
    return 1.0 / (1.0 + jnp.exp(-x))


def _inproj_kernel(x_ref, g_ref, w_ref, q_ref, k_ref, v_ref, hg_ref):
    x = x_ref[...]
    ms = jnp.mean(x * x, axis=-1, keepdims=True)
    h = (x * lax.rsqrt(ms + EPS) * g_ref[...]).astype(BF16)
    w = SB_WIDTH
    q_ref[...] = jnp.dot(h, w_ref[:, 0:w], preferred_element_type=F32)
    k_ref[...] = jnp.dot(h, w_ref[:, w:2 * w], preferred_element_type=F32)
    v_ref[...] = jnp.dot(h, w_ref[:, 2 * w:3 * w], preferred_element_type=F32)
    hg_ref[...] = jnp.dot(h, w_ref[:, 3 * w:], preferred_element_type=F32)


def _inproj(x2d, gain, w_in_bf16, tm):
    n, d = x2d.shape
    e = w_in_bf16.shape[1]
    hgw = e - 3 * SB_WIDTH
    return pl.pallas_call(
        _inproj_kernel,
        grid=(n // tm,),
        in_specs=[
            pl.BlockSpec((tm, d), lambda i: (i, 0)),
            pl.BlockSpec((1, d), lambda i: (0, 0)),
            pl.BlockSpec((d, e), lambda i: (0, 0)),
        ],
        out_specs=[
            pl.BlockSpec((tm, SB_WIDTH), lambda i: (i, 0)),
            pl.BlockSpec((tm, SB_WIDTH), lambda i: (i, 0)),
            pl.BlockSpec((tm, SB_WIDTH), lambda i: (i, 0)),
            pl.BlockSpec((tm, hgw), lambda i: (i, 0)),
        ],
        out_shape=[
            jax.ShapeDtypeStruct((n, SB_WIDTH), F32),
            jax.ShapeDtypeStruct((n, SB_WIDTH), F32),
            jax.ShapeDtypeStruct((n, SB_WIDTH), F32),
            jax.ShapeDtypeStruct((n, hgw), F32),
        ],
        compiler_params=_cparams("parallel"),
        name="inproj",
    )(x2d, gain.reshape(1, d), w_in_bf16)


def _inproj_prompt_kernel(x_ref, g_ref, wt_ref, wk_ref, whg_ref,
                          kt_ref, vt_ref, qtb_ref, kb_ref, vtb_ref, hg_ref, *, tk):
    x = x_ref[0]
    tm = x.shape[0]
    ms = jnp.mean(x * x, axis=-1, keepdims=True)
    h = (x * lax.rsqrt(ms + EPS) * g_ref[...]).astype(BF16)
    qkvt = lax.dot_general(wt_ref[...], h, NT_DIMS, preferred_element_type=F32)
    k = jnp.dot(h, wk_ref[...], preferred_element_type=F32)
    w, hd = SB_WIDTH, SB_HEAD_DIM
    for hh in range(SB_HEADS):
        r = hh * hd
        qtb_ref[0, hh] = qkvt[r:r + hd].astype(BF16)
        kt_ref[0, hh] = qkvt[w + r:w + r + hd]
        vt = qkvt[2 * w + r:2 * w + r + hd]
        vt_ref[0, hh] = vt
        for c in range(tm // tk):
            vtb_ref[0, hh, c] = vt[:, c * tk:(c + 1) * tk].astype(BF16)
        kb_ref[0, hh] = k[:, r:r + hd].astype(BF16)
    hg_ref[0] = jnp.dot(h, whg_ref[...], preferred_element_type=F32)


def _inproj_prompt(x, gain, wqkv_t, wk, whg, tm, tk):
    b, t, d = x.shape
    nh, hd, w = SB_HEADS, SB_HEAD_DIM, SB_WIDTH
    hgw = whg.shape[1]
    fixed = lambda bi, i: (0, 0)
    tspec = pl.BlockSpec((1, nh, hd, tm), lambda bi, i: (bi, 0, 0, i))
    return pl.pallas_call(
        functools.partial(_inproj_prompt_kernel, tk=tk),
        grid=(b, t // tm),
        in_specs=[
            pl.BlockSpec((1, tm, d), lambda bi, i: (bi, i, 0)),
            pl.BlockSpec((1, d), fixed),
            pl.BlockSpec((3 * w, d), fixed),
            pl.BlockSpec((d, w), fixed),
            pl.BlockSpec((d, hgw), fixed),
        ],
        out_specs=[
            tspec, tspec, tspec,
            pl.BlockSpec((1, nh, tm, hd), lambda bi, i: (bi, 0, i, 0)),
            pl.BlockSpec((1, nh, tm // tk, hd, tk), lambda bi, i: (bi, 0, i, 0, 0)),
            pl.BlockSpec((1, tm, hgw), lambda bi, i: (bi, i, 0)),
        ],
        out_shape=[
            jax.ShapeDtypeStruct((b, nh, hd, t), F32),
            jax.ShapeDtypeStruct((b, nh, hd, t), F32),
            jax.ShapeDtypeStruct((b, nh, hd, t), BF16),
            jax.ShapeDtypeStruct((b, nh, t, hd), BF16),
            jax.ShapeDtypeStruct((b, nh, t // tk, hd, tk), BF16),
            jax.ShapeDtypeStruct((b, t, hgw), F32),
        ],
        compiler_params=_cparams("parallel", "parallel"),
        name="inproj_prompt",
    )(x, gain.reshape(1, d), wqkv_t, wk, whg)


def _sb_block(z, tri, carry, before):
    lk = -_softplus(z)
    if before is not None:
        lk = jnp.where(before, lk, 0.0)
    hi, lo = _split2(lk)
    cs = (jnp.dot(hi, tri, preferred_element_type=F32)
          + jnp.dot(lo, tri, preferred_element_type=F32))
    la = z + cs + carry
    if before is not None:
        la = jnp.where(before, la, -jnp.inf)
    return jnp.exp(la), carry + cs[:, 0:1]


def _tri_incl(n):
    j = lax.broadcasted_iota(jnp.int32, (n, n), 0)
    s = lax.broadcasted_iota(jnp.int32, (n, n), 1)
    return (j >= s).astype(BF16)


def _sb_prompt_kernel(bias_ref, qt_ref, k_ref, vt_ref, ot_ref, z0_ref, hl0_ref, z1_ref, hl1_ref,
                      *, tq, tk):
    h = pl.program_id(1)
    i = pl.program_id(2)
    bias = bias_ref[h]
    qt = qt_ref[0, 0]
    nb = tq // tk
    r = lax.broadcasted_iota(jnp.int32, (tk, tk), 0)
    c = lax.broadcasted_iota(jnp.int32, (tk, tk), 1)
    triu = (c >= r).astype(BF16)
    tri2 = jnp.concatenate([triu, triu], axis=1)
    bufs = ((z0_ref, hl0_ref), (z1_ref, hl1_ref))

    def scores(g, slot, masked):
        z_ref, hl_ref = bufs[slot]
        rows = k_ref[0, 0, pl.ds(pl.multiple_of(g * tq, tq), tq), :]
        z = jnp.dot(rows, qt, preferred_element_type=F32) + bias
        sp = _softplus(z)
        if masked:
            s_loc = lax.broadcasted_iota(jnp.int32, (tq, tq), 0)
            t_loc = lax.broadcasted_iota(jnp.int32, (tq, tq), 1)
            before = s_loc < t_loc
            sp = jnp.where(before, sp, 0.0)
            z = jnp.where(before, z, -jnp.inf)
        z_ref[...] = z
        hi, lo = _split2(sp)
        for p in range(nb):
            hl_ref[p, 0:tk, :] = hi[p * tk:(p + 1) * tk]
            hl_ref[p, tk:2 * tk, :] = lo[p * tk:(p + 1) * tk]

    def weights(g, slot, carry, acc):
        z_ref, hl_ref = bufs[slot]
        a_blocks = [None] * nb
        for p in reversed(range(nb)):
            cs = jnp.dot(tri2, hl_ref[p], preferred_element_type=F32)
            la = z_ref[p * tk:(p + 1) * tk, :] - cs - carry
            a_blocks[p] = jnp.exp(la).astype(BF16)
            carry = carry + cs[0:1, :]
        a = jnp.concatenate(a_blocks, axis=0)
        return carry, acc + jnp.dot(vt_ref[0, 0, g], a, preferred_element_type=F32)

    scores(i, 0, True)

    def pair(m, c):
        g = i - 2 * m
        scores(g - 1, 1, False)
        c = weights(g, 0, *c)
        scores(g - 2, 0, False)
        return weights(g - 1, 1, *c)

    init = (jnp.zeros((1, tq), F32), jnp.zeros((SB_HEAD_DIM, tq), F32))
    c = lax.fori_loop(0, i // 2, pair, init)

    def odd_tail(c):
        scores(0, 1, False)
        return weights(0, 1, *weights(1, 0, *c))

    carry, acc = lax.cond(i % 2 == 1, odd_tail, lambda c: weights(0, 0, *c), c)
    ot_ref[0, 0] = acc


def _sb_prompt(qt, k, vt, bias, tk):
    b, h, d, t = qt.shape
    tq = vt.shape[-1]
    return pl.pallas_call(
        functools.partial(_sb_prompt_kernel, tq=tq, tk=tk),
        grid=(b, h, t // tq),
        in_specs=[
            pl.BlockSpec(memory_space=pltpu.SMEM),
            pl.BlockSpec((1, 1, d, tq), lambda bi, hi, i: (bi, hi, 0, i)),
            pl.BlockSpec((1, 1, t, d), lambda bi, hi, i: (bi, hi, 0, 0)),
            pl.BlockSpec((1, 1, t // tq, d, tq), lambda bi, hi, i: (bi, hi, 0, 0, 0)),
        ],
        out_specs=pl.BlockSpec((1, 1, d, tq), lambda bi, hi, i: (bi, hi, 0, i)),
        out_shape=jax.ShapeDtypeStruct((b, h, d, t), F32),
        scratch_shapes=[
            pltpu.VMEM((tq, tq), F32),
            pltpu.VMEM((tq // tk, 2 * tk, tq), BF16),
        ] * 2,
        compiler_params=_cparams("parallel", "parallel", "arbitrary"),
        name="sb_prompt",
    )(bias, qt, k, vt)


def _sb_sample_kernel(pt_ref, qbd_ref, bias_ref, kn_ref, vn_ref, *rest, pages_per_step, t_new):
    g = pages_per_step
    k_refs = rest[:g]
    v_refs = rest[g:2 * g]
    o_ref = rest[2 * g]
    carry_ref, acc_ref, kpad_ref, vpad_ref = rest[2 * g + 1:]
    step_id = pl.program_id(1)
    rows = qbd_ref.shape[1]
    qbd = qbd_ref[0]
    bias = bias_ref[...]
    tri = _tri_incl(PAGE_SIZE)

    def block(kpage, vpage, carry, before):
        kb = kpage.astype(BF16)
        vb = vpage.astype(BF16)
        if before is not None:
            z = lax.dot_general(qbd, kb, NT_DIMS, preferred_element_type=F32) + bias
        else:
            z = jnp.dot(qbd, kb, preferred_element_type=F32) + bias
        a, carry = _sb_block(z, tri, carry, before)
        ab = a.astype(BF16)
        if before is not None:
            return carry, jnp.dot(ab, vb, preferred_element_type=F32)
        return carry, lax.dot_general(ab, vb, NT_DIMS, preferred_element_type=F32)

    @pl.when(step_id == 0)
    def _():
        kpad_ref[...] = jnp.zeros_like(kpad_ref)
        vpad_ref[...] = jnp.zeros_like(vpad_ref)
        kpad_ref[0:kn_ref.shape[1], :] = kn_ref[0]
        vpad_ref[0:vn_ref.shape[1], :] = vn_ref[0]
        qi = lax.broadcasted_iota(jnp.int32, (rows, PAGE_SIZE), 0) % t_new
        s = lax.broadcasted_iota(jnp.int32, (rows, PAGE_SIZE), 1)
        carry, contrib = block(kpad_ref[...], vpad_ref[...], jnp.zeros((rows, 1), F32), s < qi)
        carry_ref[...] = jnp.broadcast_to(carry, carry_ref.shape)
        acc_ref[...] = contrib

    carry = carry_ref[:, 0:1]
    acc = acc_ref[...]
    for p in range(g):
        carry, contrib = block(k_refs[p][0], v_refs[p][0], carry, None)
        acc = acc + contrib
    carry_ref[...] = jnp.broadcast_to(carry, carry_ref.shape)
    acc_ref[...] = acc

    @pl.when(step_id == pl.num_programs(1) - 1)
    def _():
        o_ref[0] = acc


def _sb_sample(qbd, bias_rows, k_new, v_new, cache_k, cache_v, page_table_flat, n_pages, t_new,
               pages_per_step):
    b, rows, w = qbd.shape
    g = pages_per_step
    steps = n_pages // g

    def page_map(p):
        return lambda bi, si, pt: (pt[bi * n_pages + (n_pages - 1 - (si * g + p))], 0, 0)

    page_specs = [pl.BlockSpec((1, w, PAGE_SIZE), page_map(p)) for p in range(g)]
    grid_spec = pltpu.PrefetchScalarGridSpec(
        num_scalar_prefetch=1,
        grid=(b, steps),
        in_specs=[
            pl.BlockSpec((1, rows, w), lambda bi, si, pt: (bi, 0, 0)),
            pl.BlockSpec((rows, PAGE_SIZE), lambda bi, si, pt: (0, 0)),
            pl.BlockSpec((1, k_new.shape[1], w), lambda bi, si, pt: (bi, 0, 0)),
            pl.BlockSpec((1, v_new.shape[1], w), lambda bi, si, pt: (bi, 0, 0)),
        ] + page_specs + page_specs,
        out_specs=pl.BlockSpec((1, rows, w), lambda bi, si, pt: (bi, 0, 0)),
        scratch_shapes=[
            pltpu.VMEM((rows, PAGE_SIZE), F32),
            pltpu.VMEM((rows, w), F32),
            pltpu.VMEM((PAGE_SIZE, w), F32),
            pltpu.VMEM((PAGE_SIZE, w), F32),
        ],
    )
    return pl.pallas_call(
        functools.partial(_sb_sample_kernel, pages_per_step=g, t_new=t_new),
        grid_spec=grid_spec,
        out_shape=jax.ShapeDtypeStruct((b, rows, w), F32),
        compiler_params=_cparams("parallel", "arbitrary"),
        name="sb_sample",
    )(page_table_flat, qbd, bias_rows, k_new, v_new, *([cache_k] * g), *([cache_v] * g))


def _lower_bound(lb_logits, layer):
    m = jnp.max(lb_logits, axis=0, keepdims=True)
    e = jnp.exp(lb_logits - m)
    return jnp.sum(e[0:layer + 1], axis=0, keepdims=True) / jnp.sum(e, axis=0, keepdims=True)


def _hg_out(o, gain, gb):
    ms = jnp.mean(o * o, axis=-1, keepdims=True)
    return (o * lax.rsqrt(ms + EPS) * gain) * (gb * _sigmoid(gb))


def _hgrn_prompt_kernel(lbl_ref, gain_ref, qb_ref, fb_ref, ib_ref, gb_ref, o_ref, s_ref, st_ref,
                        *, layer, n_chunks):
    c = HG_CHUNK
    t_id = pl.program_id(2)

    @pl.when(t_id == 0)
    def _():
        st_ref[...] = jnp.zeros_like(st_ref)

    lb = _lower_bound(lbl_ref[...], layer)
    gain = gain_ref[...]
    ti = lax.broadcasted_iota(jnp.int32, (c, c), 0)
    si = lax.broadcasted_iota(jnp.int32, (c, c), 1)
    ltri = (ti >= si).astype(BF16)
    rowc = lax.broadcasted_iota(jnp.int32, (c, HG_DK), 0)
    rows = lax.broadcasted_iota(jnp.int32, (HG_SUB, HG_DK), 0)

    def chunk(n, _):
        sl = pl.ds(pl.multiple_of(n * c, c), c)
        f = lb + (1.0 - lb) * _sigmoid(fb_ref[0, sl, :])
        logf = jnp.log(f)
        kk = 1.0 - f
        qx = qb_ref[0, sl, :]
        qq = qx * _sigmoid(qx)
        v = ib_ref[0, sl, :]
        vb = v.astype(BF16)
        p1, p2, p3 = _split3(logf)
        gcum = (jnp.dot(ltri, p1, preferred_element_type=F32)
                + jnp.dot(ltri, p2, preferred_element_type=F32)
                + jnp.dot(ltri, p3, preferred_element_type=F32))
        g_last = gcum[c - 1:c, :]
        st = st_ref[...]
        o = lax.dot_general((qq * jnp.exp(gcum)).astype(BF16), st.astype(BF16), NT_DIMS,
                            preferred_element_type=F32)
        parts = [jnp.zeros((HG_SUB, HG_DV), F32)]
        for i in range(1, c // HG_SUB):
            lo = i * HG_SUB
            r = gcum[lo - 1:lo, :]
            qh = qq[lo:lo + HG_SUB] * jnp.exp(gcum[lo:lo + HG_SUB] - r)
            kh = kk * jnp.exp(jnp.where(rowc < lo, r - gcum, -jnp.inf))
            att = lax.dot_general(qh.astype(BF16), kh.astype(BF16), NT_DIMS,
                                  preferred_element_type=F32)
            parts.append(jnp.dot(att.astype(BF16), vb, preferred_element_type=F32))
        diag = []
        for i in range(c // HG_SUB):
            lo = i * HG_SUB
            gs = gcum[lo:lo + HG_SUB]
            qs = qq[lo:lo + HG_SUB]
            ks = kk[lo:lo + HG_SUB]
            vs = v[lo:lo + HG_SUB]
            od = jnp.zeros((HG_SUB, HG_DV), F32)
            for s in range(HG_SUB):
                e = jnp.exp(jnp.where(rows >= s, gs - gs[s:s + 1, :], -jnp.inf))
                a = jnp.sum(qs * (ks[s:s + 1, :] * e), axis=1, keepdims=True)
                od = od + a * vs[s:s + 1, :]
            diag.append(od + parts[i])
        o = o + jnp.concatenate(diag, axis=0)
        o_ref[0, sl, :] = _hg_out(o, gain, gb_ref[0, sl, :])
        ke = kk * jnp.exp(g_last - gcum)
        st_ref[...] = st * jnp.exp(g_last) + lax.dot_general(
            vb, ke.astype(BF16), TN_DIMS, preferred_element_type=F32)
        return 0

    lax.fori_loop(0, n_chunks, chunk, 0)

    @pl.when(t_id == pl.num_programs(2) - 1)
    def _():
        s_ref[0, 0] = st_ref[...].T


def _hgrn_prompt(hg, lb_logits, gain, layer, tt):
    b, t, _ = hg.shape
    nh = HG_HEADS

    def col(which):
        return pl.BlockSpec((1, tt, HG_DK), lambda bi, hi, ti: (bi, ti, which * nh + hi))

    return pl.pallas_call(
        functools.partial(_hgrn_prompt_kernel, layer=layer, n_chunks=tt // HG_CHUNK),
        grid=(b, nh, t // tt),
        in_specs=[
            pl.BlockSpec((lb_logits.shape[0], HG_DK), lambda bi, hi, ti: (0, hi)),
            pl.BlockSpec((1, HG_DV), lambda bi, hi, ti: (0, 0)),
            col(0), col(1), col(2), col(3),
        ],
        out_specs=[
            pl.BlockSpec((1, tt, HG_DV), lambda bi, hi, ti: (bi, ti, hi)),
            pl.BlockSpec((1, 1, HG_DK, HG_DV), lambda bi, hi, ti: (bi, hi, 0, 0)),
        ],
        out_shape=[
            jax.ShapeDtypeStruct((b, t, HG_WIDTH), F32),
            jax.ShapeDtypeStruct((b, nh, HG_DK, HG_DV), F32),
        ],
        scratch_shapes=[pltpu.VMEM((HG_DV, HG_DK), F32)],
        compiler_params=_cparams("parallel", "parallel", "arbitrary"),
        name="hgrn_prompt",
    )(lb_logits, gain.reshape(1, HG_DV), hg, hg, hg, hg)


def _hgrn_sample_kernel(lbl_ref, gain_ref, hg_ref, s0_ref, o_ref, s_ref, *, layer, t_new):
    w = HG_WIDTH
    lb_all = _lower_bound(lbl_ref[...], layer)
    gain = gain_ref[...]
    x = hg_ref[0]
    outs = []
    for h in range(HG_HEADS):
        c0 = h * HG_DK
        lb = lb_all[:, c0:c0 + HG_DK]
        qx = x[:, c0:c0 + HG_DK]
        f = lb + (1.0 - lb) * _sigmoid(x[:, w + c0:w + c0 + HG_DK])
        v = x[:, 2 * w + c0:2 * w + c0 + HG_DV]
        gb = x[:, 3 * w + c0:3 * w + c0 + HG_DV]
        stack = jnp.concatenate(
            [f, 1.0 - f, qx * _sigmoid(qx),
             jnp.zeros((HG_DK - 3 * t_new, HG_DK), F32)], axis=0)
        cols = stack.T
        s = s0_ref[0, h]
        o_rows = []
        for t in range(t_new):
            fc = cols[:, t:t + 1]
            kc = cols[:, t_new + t:t_new + t + 1]
            qc = cols[:, 2 * t_new + t:2 * t_new + t + 1]
            s = fc * s + kc * v[t:t + 1, :]
            o_rows.append(jnp.sum(qc * s, axis=0, keepdims=True))
        s_ref[0, h] = s
        outs.append(_hg_out(jnp.concatenate(o_rows, axis=0), gain, gb))
    o_ref[0] = jnp.concatenate(outs, axis=1)


def _hgrn_sample(hg, state, lb_logits, gain, layer):
    b, t_new, e = hg.shape
    nh = HG_HEADS
    return pl.pallas_call(
        functools.partial(_hgrn_sample_kernel, layer=layer, t_new=t_new),
        grid=(b,),
        in_specs=[
            pl.BlockSpec(lb_logits.shape, lambda bi: (0, 0)),
            pl.BlockSpec((1, HG_DV), lambda bi: (0, 0)),
            pl.BlockSpec((1, t_new, e), lambda bi: (bi, 0, 0)),
            pl.BlockSpec((1, nh, HG_DK, HG_DV), lambda bi: (bi, 0, 0, 0)),
        ],
        out_specs=[
            pl.BlockSpec((1, t_new, HG_WIDTH), lambda bi: (bi, 0, 0)),
            pl.BlockSpec((1, nh, HG_DK, HG_DV), lambda bi: (bi, 0, 0, 0)),
        ],
        out_shape=[
            jax.ShapeDtypeStruct((b, t_new, HG_WIDTH), F32),
            jax.ShapeDtypeStruct((b, nh, HG_DK, HG_DV), F32),
        ],
        compiler_params=_cparams("parallel"),
        name="hgrn_sample",
    )(lb_logits, gain.reshape(1, HG_DV), hg, state)


def _route(logits):
    lane = lax.broadcasted_iota(jnp.int32, logits.shape, 1)
    neg = -jnp.inf
    big = ROUTER_LANES

    def top(mask):
        val = jnp.max(jnp.where(mask, logits, neg), axis=1, keepdims=True)
        idx = jnp.min(jnp.where(mask & (logits == val), lane, big), axis=1, keepdims=True)
        return val, idx

    is_group = lane < N_GROUPS
    gmax, gidx = top(is_group)
    p_top = 1.0 / jnp.sum(jnp.where(is_group, jnp.exp(logits - gmax), 0.0), axis=1, keepdims=True)
    lo = N_GROUPS + gidx * EXPERTS_PER_GROUP
    in_group = (lane >= lo) & (lane < lo + EXPERTS_PER_GROUP)
    v1, i1 = top(in_group)
    v2, i2 = top(in_group & (lane != i1))
    e2 = jnp.exp(v2 - v1)
    w1 = p_top / (1.0 + e2)
    w2 = p_top * e2 / (1.0 + e2)
    return jnp.where(lane == i1, w1, 0.0) + jnp.where(lane == i2, w2, 0.0)


def _post_kernel(x_ref, oa_ref, ob_ref, sbg_ref, wo_ref, ng_ref, wr_ref, br_ref,
                 x1_ref, h2_ref, gate_ref, *, oa_transposed):
    w = SB_WIDTH
    if oa_transposed:
        oa = jnp.concatenate([oa_ref[0, hh] for hh in range(SB_HEADS)], axis=0)
        ms = jnp.mean(oa * oa, axis=0, keepdims=True)
        oa = (oa * lax.rsqrt(ms + EPS) * sbg_ref[...]).astype(BF16)
        mixed = lax.dot_general(oa, wo_ref[0:w, :], TN_DIMS, preferred_element_type=F32)
    else:
        oa = oa_ref[...]
        ms = jnp.mean(oa * oa, axis=-1, keepdims=True)
        oa = (oa * lax.rsqrt(ms + EPS) * sbg_ref[...]).astype(BF16)
        mixed = jnp.dot(oa, wo_ref[0:w, :], preferred_element_type=F32)
    mixed = mixed + jnp.dot(ob_ref[...].astype(BF16), wo_ref[w:, :], preferred_element_type=F32)
    x1 = x_ref[...] + mixed
    x1_ref[...] = x1
    ms = jnp.mean(x1 * x1, axis=-1, keepdims=True)
    h2 = x1 * lax.rsqrt(ms + EPS) * ng_ref[...]
    h2_ref[...] = h2.astype(BF16)
    hh, hl = _split2(h2)
    logits = (jnp.dot(hh, wr_ref[0], preferred_element_type=F32)
              + jnp.dot(hl, wr_ref[0], preferred_element_type=F32)
              + jnp.dot(hh, wr_ref[1], preferred_element_type=F32)) + br_ref[...]
    gate_ref[...] = _route(logits)


def _post(x2d, oa, ob, sb_gain, w_out_bf16, ffn_gain, w_router_hl, b_router, tm):
    n, d = x2d.shape
    row = lambda i: (i, 0)
    fixed = lambda i: (0, 0)
    oa_transposed = oa.ndim == 4
    if oa_transposed:
        per_b = oa.shape[3] // tm
        oa_spec = pl.BlockSpec((1, SB_HEADS, SB_HEAD_DIM, tm),
                               lambda i: (i // per_b, 0, 0, i % per_b))
        sb_gain = sb_gain.reshape(SB_WIDTH, 1)
    else:
        oa_spec = pl.BlockSpec((tm, SB_WIDTH), row)
        sb_gain = sb_gain.reshape(1, SB_WIDTH)
    return pl.pallas_call(
        functools.partial(_post_kernel, oa_transposed=oa_transposed),
        grid=(n // tm,),
        in_specs=[
            pl.BlockSpec((tm, d), row),
            oa_spec,
            pl.BlockSpec((tm, HG_WIDTH), row),
            pl.BlockSpec(sb_gain.shape, fixed),
            pl.BlockSpec(w_out_bf16.shape, fixed),
            pl.BlockSpec((1, d), fixed),
            pl.BlockSpec(w_router_hl.shape, lambda i: (0, 0, 0)),
            pl.BlockSpec((1, ROUTER_LANES), fixed),
        ],
        out_specs=[
            pl.BlockSpec((tm, d), row),
            pl.BlockSpec((tm, d), row),
            pl.BlockSpec((tm, ROUTER_LANES), row),
        ],
        out_shape=[
            jax.ShapeDtypeStruct((n, d), F32),
            jax.ShapeDtypeStruct((n, d), BF16),
            jax.ShapeDtypeStruct((n, ROUTER_LANES), F32),
        ],
        compiler_params=_cparams("parallel"),
        name="post",
    )(x2d, oa, ob, sb_gain, w_out_bf16, ffn_gain.reshape(1, d), w_router_hl, b_router)


def _moe_kernel(x1_ref, h2_ref, gate_ref, wg_ref, wu_ref, wd_ref, fg_ref, y_ref, acc_ref):
    e = pl.program_id(1)

    @pl.when(e == 0)
    def _():
        acc_ref[...] = jnp.zeros_like(acc_ref)

    h2 = h2_ref[...]
    gate = gate_ref[...]
    lane = lax.broadcasted_iota(jnp.int32, gate.shape, 1)
    gcol = jnp.sum(jnp.where(lane == N_GROUPS + e, gate, 0.0), axis=1, keepdims=True)
    a = jnp.dot(h2, wg_ref[0], preferred_element_type=F32)
    u = jnp.dot(h2, wu_ref[0], preferred_element_type=F32)
    act = (a * _sigmoid(a)) * u * gcol
    acc_ref[...] += jnp.dot(act.astype(BF16), wd_ref[0], preferred_element_type=F32)

    @pl.when(e == pl.num_programs(1) - 1)
    def _():
        x2 = x1_ref[...] + acc_ref[...]
        ms = jnp.mean(x2 * x2, axis=-1, keepdims=True)
        y_ref[...] = x2 * lax.rsqrt(ms + EPS) * fg_ref[...]


def _moe(x1, h2, gate, wg, wu, wd, final_gain, tm):
    n, d = x1.shape
    ne, _, df = wg.shape
    row = lambda i, e: (i, 0)
    return pl.pallas_call(
        _moe_kernel,
        grid=(n // tm, ne),
        in_specs=[
            pl.BlockSpec((tm, d), row),
            pl.BlockSpec((tm, d), row),
            pl.BlockSpec((tm, ROUTER_LANES), row),
            pl.BlockSpec((1, d, df), lambda i, e: (e, 0, 0)),
            pl.BlockSpec((1, d, df), lambda i, e: (e, 0, 0)),
            pl.BlockSpec((1, df, d), lambda i, e: (e, 0, 0)),
            pl.BlockSpec((1, d), lambda i, e: (0, 0)),
        ],
        out_specs=pl.BlockSpec((tm, d), row),
        out_shape=jax.ShapeDtypeStruct((n, d), F32),
        scratch_shapes=[pltpu.VMEM((tm, d), F32)],
        compiler_params=_cparams("parallel", "arbitrary"),
        name="moe",
    )(x1, h2, gate, wg, wu, wd, final_gain.reshape(1, d))


def _tile(n, pref):
    return pref if n % pref == 0 else n


def _router_params(w_rg, b_rg, w_re, b_re):
    d = w_rg.shape[0]
    pad = ROUTER_LANES - N_GROUPS - N_EXPERTS
    w = jnp.concatenate([w_rg, w_re, jnp.zeros((d, pad), F32)], axis=1)
    hi = w.astype(BF16)
    lo = (w - hi.astype(F32)).astype(BF16)
    b = jnp.concatenate([b_rg, b_re, jnp.zeros((pad,), F32)]).reshape(1, ROUTER_LANES)
    return jnp.stack([hi, lo]), b


def _ffn(x2d, oa, ob, sb_gain, w_out_bf16, ffn_gain, w_router_hl, b_router, wg, wu, wd, final_gain):
    n = x2d.shape[0]
    x1, h2, gate = _post(x2d, oa, ob, sb_gain, w_out_bf16, ffn_gain, w_router_hl, b_router,
                         _tile(n, 512))
    return _moe(x1, h2, gate, wg, wu, wd, final_gain, _tile(n, 1024))


def kernel(x_prompt, x_sample, cache_k, cache_v, state_hgrn, page_table, norm_attn, w_in,
           sb_logit_bias, sb_norm, hg_lb_logits, hg_norm, w_out, norm_ffn, w_router_group,
           b_router_group, w_router_expert, b_router_expert, w_exp_gate, w_exp_up, w_exp_down,
           norm_final):
    depth = w_in.shape[0]
    assert depth == 1, "single-layer trunk"
    l = 0
    bp, t, d = x_prompt.shape
    db, tn, _ = x_sample.shape
    nh, hd = SB_HEADS, SB_HEAD_DIM
    n_pages = page_table.shape[1]

    w_in_b = w_in[l].astype(BF16)
    w_out_b = w_out[l].astype(BF16)
    wr_hl, b_r = _router_params(w_router_group[l], b_router_group[l], w_router_expert[l],
                                b_router_expert[l])
    df = w_exp_gate.shape[-1]
    wg = w_exp_gate[l].reshape(N_EXPERTS, d, df).astype(BF16)
    wu = w_exp_up[l].reshape(N_EXPERTS, d, df).astype(BF16)
    wd = w_exp_down[l].reshape(N_EXPERTS, df, d).astype(BF16)
    bias = sb_logit_bias[l].astype(F32)

    xp = x_prompt.reshape(bp * t, d)
    w3 = 3 * SB_WIDTH
    wq_scaled = w_in[l][:, :SB_WIDTH] * SB_SCALE
    wqkv_t = jnp.concatenate([wq_scaled, w_in[l][:, SB_WIDTH:w3]], axis=1).T.astype(BF16)
    ktp, vtp, qtb, kb, vtb, hgp = _inproj_prompt(
        x_prompt, norm_attn[l], wqkv_t, w_in_b[:, SB_WIDTH:2 * SB_WIDTH], w_in_b[:, w3:],
        SB_QUERY_BLOCK, SB_QUERY_BLOCK)
    oatp = _sb_prompt(qtb, kb, vtb, bias, SB_KEY_BLOCK)
    obp, sp = _hgrn_prompt(hgp, hg_lb_logits, hg_norm[l], l, 512)
    yp = _ffn(xp, oatp, obp.reshape(bp * t, HG_WIDTH), sb_norm[l], w_out_b, norm_ffn[l], wr_hl,
              b_r, wg, wu, wd, norm_final)
    kp = ktp.transpose(0, 3, 1, 2)[None]
    vp = vtp.transpose(0, 3, 1, 2)[None]

    xs = x_sample.reshape(db * tn, d)
    qs, ks, vs, hgs = _inproj(xs, norm_attn[l], w_in_b, 256)
    head_of_lane = jnp.arange(SB_WIDTH) // hd
    onehot = (head_of_lane[None, :] == jnp.arange(nh)[:, None]).astype(F32)
    qbd = (qs.reshape(db, 1, tn, SB_WIDTH) * SB_SCALE) * onehot[None, :, None, :]
    qbd = qbd.reshape(db, nh * tn, SB_WIDTH).astype(BF16)
    bias_rows = jnp.broadcast_to(jnp.repeat(bias, tn)[:, None], (nh * tn, PAGE_SIZE))
    pad = ((0, 0), (0, 8 - tn), (0, 0))
    kn = jnp.pad(ks.reshape(db, tn, SB_WIDTH), pad)
    vn = jnp.pad(vs.reshape(db, tn, SB_WIDTH), pad)
    ck = cache_k[l].transpose(0, 2, 3, 1).reshape(-1, SB_WIDTH, PAGE_SIZE)
    cv = cache_v[l].transpose(0, 2, 3, 1).reshape(-1, SB_WIDTH, PAGE_SIZE)
    o_full = _sb_sample(qbd, bias_rows, kn, vn, ck, cv, page_table.reshape(-1), n_pages, tn, 16)
    oas = jnp.sum(o_full.reshape(db, nh, tn, SB_WIDTH) * onehot[None, :, None, :], axis=1)
    obs, ss = _hgrn_sample(hgs.reshape(db, tn, -1), state_hgrn[l], hg_lb_logits, hg_norm[l], l)
    ys = _ffn(xs, oas.reshape(db * tn, SB_WIDTH), obs.reshape(db * tn, HG_WIDTH), sb_norm[l],
              w_out_b, norm_ffn[l], wr_hl, b_r, wg, wu, wd, norm_final)

    sdt = state_hgrn.dtype
    return (yp.reshape(bp, t, d), ys.reshape(db, tn, d), kp, vp, sp.astype(sdt)[None],
            ks.reshape(1, db, tn, nh, hd), vs.reshape(1, db, tn, nh, hd), ss.astype(sdt)[None])
```

```python
import functools

import jax
import jax.numpy as jnp
from jax import lax
from jax.experimental import pallas as pl
from jax.experimental.pallas import tpu as pltpu

F32 = jnp.float32
BF16 = jnp.bfloat16
EPS = 1e-6
NEG_LOG2E = -1.4426950408889634

SB_HEADS = 8
SB_HEAD_DIM = 64
SB_WIDTH = SB_HEADS * SB_HEAD_DIM
SB_SCALE = SB_HEAD_DIM ** -0.5
HG_HEADS = 4
HG_DK = 128
HG_DV = 128
HG_WIDTH = HG_HEADS * HG_DK
HG_CHUNK = 64
HG_SUB = 16
N_GROUPS = 4
EXPERTS_PER_GROUP = 8
N_EXPERTS = N_GROUPS * EXPERTS_PER_GROUP
ROUTER_LANES = 128
PAGE_SIZE = 128
SB_KEY_BLOCK = 128
SB_QUERY_BLOCK = 512

VMEM_LIMIT_BYTES = 56 * 1024 * 1024

NT_DIMS = (((1,), (1,)), ((), ()))
TN_DIMS = (((0,), (0,)), ((), ()))


def _cparams(*sem):
    return pltpu.CompilerParams(dimension_semantics=sem, vmem_limit_bytes=VMEM_LIMIT_BYTES)


def _split2(x):
    hi = x.astype(BF16)
    lo = (x - hi.astype(F32)).astype(BF16)
    return hi, lo


def _split3(x):
    hi = x.astype(BF16)
    r = x - hi.astype(F32)
    mid = r.astype(BF16)
    lo = (r - mid.astype(F32)).astype(BF16)
    return hi, mid, lo


def _softplus(z):
    return jnp.maximum(z, 0.0) + jnp.log(1.0 + jnp.exp2(jnp.abs(z) * NEG_LOG2E))


def _sigmoid(x):
    return 1.0 / (1.0 + jnp.exp(-x))


def _inproj_kernel(x_ref, g_ref, w_ref, q_ref, k_ref, v_ref, hg_ref):
    x = x_ref[...]
    ms = jnp.mean(x * x, axis=-1, keepdims=True)
    h = (x * lax.rsqrt(ms + EPS) * g_ref[...]).astype(BF16)
    w = SB_WIDTH
    q_ref[...] = jnp.dot(h, w_ref[:, 0:w], preferred_element_type=F32)
    k_ref[...] = jnp.dot(h, w_ref[:, w:2 * w], preferred_element_type=F32)
    v_ref[...] = jnp.dot(h, w_ref[:, 2 * w:3 * w], preferred_element_type=F32)
    hg_ref[...] = jnp.dot(h, w_ref[:, 3 * w:], preferred_element_type=F32)


def _inproj(x2d, gain, w_in_bf16, tm):
    n, d = x2d.shape
    e = w_in_bf16.shape[1]
    hgw = e - 3 * SB_WIDTH
    return pl.pallas_call(
        _inproj_kernel,
        grid=(n // tm,),
        in_specs=[
            pl.BlockSpec((tm, d), lambda i: (i, 0)),
            pl.BlockSpec((1, d), lambda i: (0, 0)),
            pl.BlockSpec((d, e), lambda i: (0, 0)),
        ],
        out_specs=[
            pl.BlockSpec((tm, SB_WIDTH), lambda i: (i, 0)),
            pl.BlockSpec((tm, SB_WIDTH), lambda i: (i, 0)),
            pl.BlockSpec((tm, SB_WIDTH), lambda i: (i, 0)),
            pl.BlockSpec((tm, hgw), lambda i: (i, 0)),
        ],
        out_shape=[
            jax.ShapeDtypeStruct((n, SB_WIDTH), F32),
            jax.ShapeDtypeStruct((n, SB_WIDTH), F32),
            jax.ShapeDtypeStruct((n, SB_WIDTH), F32),
            jax.ShapeDtypeStruct((n, hgw), F32),
        ],
        compiler_params=_cparams("parallel"),
        name="inproj",
    )(x2d, gain.reshape(1, d), w_in_bf16)


def _inproj_prompt_kernel(x_ref, g_ref, wt_ref, wk_ref, whg_ref,
                          kt_ref, vt_ref, qtb_ref, kb_ref, vtb_ref, hg_ref, *, tk):
    x = x_ref[0]
    tm = x.shape[0]
    ms = jnp.mean(x * x, axis=-1, keepdims=True)
    h = (x * lax.rsqrt(ms + EPS) * g_ref[...]).astype(BF16)
    qkvt = lax.dot_general(wt_ref[...], h, NT_DIMS, preferred_element_type=F32)
    k = jnp.dot(h, wk_ref[...], preferred_element_type=F32)
    w, hd = SB_WIDTH, SB_HEAD_DIM
    for hh in range(SB_HEADS):
        r = hh * hd
        qtb_ref[0, hh] = qkvt[r:r + hd].astype(BF16)
        kt_ref[0, hh] = qkvt[w + r:w + r + hd]
        vt = qkvt[2 * w + r:2 * w + r + hd]
        vt_ref[0, hh] = vt
        for c in range(tm // tk):
            vtb_ref[0, hh, c] = vt[:, c * tk:(c + 1) * tk].astype(BF16)
        kb_ref[0, hh] = k[:, r:r + hd].astype(BF16)
    hg_ref[0] = jnp.dot(h, whg_ref[...], preferred_element_type=F32)


def _inproj_prompt(x, gain, wqkv_t, wk, whg, tm, tk):
    b, t, d = x.shape
    nh, hd, w = SB_HEADS, SB_HEAD_DIM, SB_WIDTH
    hgw = whg.shape[1]
    fixed = lambda bi, i: (0, 0)
    tspec = pl.BlockSpec((1, nh, hd, tm), lambda bi, i: (bi, 0, 0, i))
    return pl.pallas_call(
        functools.partial(_inproj_prompt_kernel, tk=tk),
        grid=(b, t // tm),
        in_specs=[
            pl.BlockSpec((1, tm, d), lambda bi, i: (bi, i, 0)),
            pl.BlockSpec((1, d), fixed),
            pl.BlockSpec((3 * w, d), fixed),
            pl.BlockSpec((d, w), fixed),
            pl.BlockSpec((d, hgw), fixed),
        ],
        out_specs=[
            tspec, tspec, tspec,
            pl.BlockSpec((1, nh, tm, hd), lambda bi, i: (bi, 0, i, 0)),
            pl.BlockSpec((1, nh, tm // tk, hd, tk), lambda bi, i: (bi, 0, i, 0, 0)),
            pl.BlockSpec((1, tm, hgw), lambda bi, i: (bi, i, 0)),
        ],
        out_shape=[
            jax.ShapeDtypeStruct((b, nh, hd, t), F32),
            jax.ShapeDtypeStruct((b, nh, hd, t), F32),
            jax.ShapeDtypeStruct((b, nh, hd, t), BF16),
            jax.ShapeDtypeStruct((b, nh, t, hd), BF16),
            jax.ShapeDtypeStruct((b, nh, t // tk, hd, tk), BF16),
            jax.ShapeDtypeStruct((b, t, hgw), F32),
        ],
        compiler_params=_cparams("parallel", "parallel"),
        name="inproj_prompt",
    )(x, gain.reshape(1, d), wqkv_t, wk, whg)


def _sb_row_weights(z, tri2, carry, n, before=None):
    sp = _softplus(z)
    if before is not None:
        sp = jnp.where(before, sp, 0.0)
        z = jnp.where(before, z, -jnp.inf)
    hi, lo = _split2(sp)
    blocks = []
    for p in range(z.shape[1] // n):
        sl = slice(p * n, (p + 1) * n)
        cs = jnp.dot(jnp.concatenate([hi[:, sl], lo[:, sl]], axis=1), tri2,
                     preferred_element_type=F32)
        blocks.append(jnp.exp(z[:, sl] - cs - carry).astype(BF16))
        carry = carry + cs[:, 0:1]
    return jnp.concatenate(blocks, axis=1), carry


def _tri_incl(n):
    j = lax.broadcasted_iota(jnp.int32, (n, n), 0)
    s = lax.broadcasted_iota(jnp.int32, (n, n), 1)
    return (j >= s).astype(BF16)


def _sb_prompt_kernel(bias_ref, qt_ref, k_ref, vt_ref, ot_ref, z0_ref, hl0_ref, z1_ref, hl1_ref,
                      *, tq, tk):
    h = pl.program_id(1)
    i = pl.program_id(2)
    bias = bias_ref[h]
    qt = qt_ref[0, 0]
    nb = tq // tk
    r = lax.broadcasted_iota(jnp.int32, (tk, tk), 0)
    c = lax.broadcasted_iota(jnp.int32, (tk, tk), 1)
    triu = (c >= r).astype(BF16)
    tri2 = jnp.concatenate([triu, triu], axis=1)
    bufs = ((z0_ref, hl0_ref), (z1_ref, hl1_ref))

    def scores(g, slot, masked):
        z_ref, hl_ref = bufs[slot]
        rows = k_ref[0, 0, pl.ds(pl.multiple_of(g * tq, tq), tq), :]
        z = jnp.dot(rows, qt, preferred_element_type=F32) + bias
        sp = _softplus(z)
        if masked:
            s_loc = lax.broadcasted_iota(jnp.int32, (tq, tq), 0)
            t_loc = lax.broadcasted_iota(jnp.int32, (tq, tq), 1)
            before = s_loc < t_loc
            sp = jnp.where(before, sp, 0.0)
            z = jnp.where(before, z, -jnp.inf)
        z_ref[...] = z
        hi, lo = _split2(sp)
        for p in range(nb):
            hl_ref[p, 0:tk, :] = hi[p * tk:(p + 1) * tk]
            hl_ref[p, tk:2 * tk, :] = lo[p * tk:(p + 1) * tk]

    def weights(g, slot, carry, acc):
        z_ref, hl_ref = bufs[slot]
        a_blocks = [None] * nb
        for p in reversed(range(nb)):
            cs = jnp.dot(tri2, hl_ref[p], preferred_element_type=F32)
            la = z_ref[p * tk:(p + 1) * tk, :] - cs - carry
            a_blocks[p] = jnp.exp(la).astype(BF16)
            carry = carry + cs[0:1, :]
        a = jnp.concatenate(a_blocks, axis=0)
        return carry, acc + jnp.dot(vt_ref[0, 0, g], a, preferred_element_type=F32)

    scores(i, 0, True)

    def pair(m, c):
        g = i - 2 * m
        scores(g - 1, 1, False)
        c = weights(g, 0, *c)
        scores(g - 2, 0, False)
        return weights(g - 1, 1, *c)

    init = (jnp.zeros((1, tq), F32), jnp.zeros((SB_HEAD_DIM, tq), F32))
    c = lax.fori_loop(0, i // 2, pair, init)

    def odd_tail(c):
        scores(0, 1, False)
        return weights(0, 1, *weights(1, 0, *c))

    carry, acc = lax.cond(i % 2 == 1, odd_tail, lambda c: weights(0, 0, *c), c)
    ot_ref[0, 0] = acc


def _sb_prompt(qt, k, vt, bias, tk):
    b, h, d, t = qt.shape
    tq = vt.shape[-1]
    return pl.pallas_call(
        functools.partial(_sb_prompt_kernel, tq=tq, tk=tk),
        grid=(b, h, t // tq),
        in_specs=[
            pl.BlockSpec(memory_space=pltpu.SMEM),
            pl.BlockSpec((1, 1, d, tq), lambda bi, hi, i: (bi, hi, 0, i)),
            pl.BlockSpec((1, 1, t, d), lambda bi, hi, i: (bi, hi, 0, 0)),
            pl.BlockSpec((1, 1, t // tq, d, tq), lambda bi, hi, i: (bi, hi, 0, 0, 0)),
        ],
        out_specs=pl.BlockSpec((1, 1, d, tq), lambda bi, hi, i: (bi, hi, 0, i)),
        out_shape=jax.ShapeDtypeStruct((b, h, d, t), F32),
        scratch_shapes=[
            pltpu.VMEM((tq, tq), F32),
            pltpu.VMEM((tq // tk, 2 * tk, tq), BF16),
        ] * 2,
        compiler_params=_cparams("parallel", "parallel", "arbitrary"),
        name="sb_prompt",
    )(bias, qt, k, vt)


def _sb_sample_kernel(pt_ref, qbd_ref, bias_ref, kn_ref, vn_ref, *rest, pages_per_step, t_new):
    g = pages_per_step
    k_refs = rest[:g]
    v_refs = rest[g:2 * g]
    o_ref = rest[2 * g]
    carry_ref, acc_ref, kpad_ref, vpad_ref = rest[2 * g + 1:]
    step_id = pl.program_id(1)
    rows = qbd_ref.shape[1]
    n = PAGE_SIZE
    qbd = qbd_ref[0]
    bias = bias_ref[...]
    tri = _tri_incl(n)
    tri2 = jnp.concatenate([tri, tri], axis=0)

    @pl.when(step_id == 0)
    def _():
        kpad_ref[...] = jnp.zeros_like(kpad_ref)
        vpad_ref[...] = jnp.zeros_like(vpad_ref)
        kpad_ref[0:kn_ref.shape[1], :] = kn_ref[0]
        vpad_ref[0:vn_ref.shape[1], :] = vn_ref[0]
        qi = lax.broadcasted_iota(jnp.int32, (rows, n), 0) % t_new
        s = lax.broadcasted_iota(jnp.int32, (rows, n), 1)
        z = lax.dot_general(qbd, kpad_ref[...].astype(BF16), NT_DIMS,
                            preferred_element_type=F32) + bias
        a, carry = _sb_row_weights(z, tri2, jnp.zeros((rows, 1), F32), n, s < qi)
        carry_ref[...] = jnp.broadcast_to(carry, carry_ref.shape)
        acc_ref[...] = jnp.dot(a, vpad_ref[...].astype(BF16), preferred_element_type=F32)

    kcat = jnp.concatenate([k_refs[p][0].astype(BF16) for p in range(g)], axis=1)
    vcat = jnp.concatenate([v_refs[p][0].astype(BF16) for p in range(g)], axis=1)
    z = jnp.dot(qbd, kcat, preferred_element_type=F32) + jnp.tile(bias, (1, g))
    a, carry = _sb_row_weights(z, tri2, carry_ref[:, 0:1], n)
    acc = acc_ref[...] + lax.dot_general(a, vcat, NT_DIMS, preferred_element_type=F32)
    carry_ref[...] = jnp.broadcast_to(carry, carry_ref.shape)
    acc_ref[...] = acc

    @pl.when(step_id == pl.num_programs(1) - 1)
    def _():
        o_ref[0] = acc


def _sb_sample(qbd, bias_rows, k_new, v_new, cache_k, cache_v, page_table_flat, n_pages, t_new,
               pages_per_step):
    b, rows, w = qbd.shape
    g = pages_per_step
    steps = n_pages // g

    def page_map(p):
        return lambda bi, si, pt: (pt[bi * n_pages + (n_pages - 1 - (si * g + p))], 0, 0)

    page_specs = [pl.BlockSpec((1, w, PAGE_SIZE), page_map(p)) for p in range(g)]
    grid_spec = pltpu.PrefetchScalarGridSpec(
        num_scalar_prefetch=1,
        grid=(b, steps),
        in_specs=[
            pl.BlockSpec((1, rows, w), lambda bi, si, pt: (bi, 0, 0)),
            pl.BlockSpec((rows, PAGE_SIZE), lambda bi, si, pt: (0, 0)),
            pl.BlockSpec((1, k_new.shape[1], w), lambda bi, si, pt: (bi, 0, 0)),
            pl.BlockSpec((1, v_new.shape[1], w), lambda bi, si, pt: (bi, 0, 0)),
        ] + page_specs + page_specs,
        out_specs=pl.BlockSpec((1, rows, w), lambda bi, si, pt: (bi, 0, 0)),
        scratch_shapes=[
            pltpu.VMEM((rows, PAGE_SIZE), F32),
            pltpu.VMEM((rows, w), F32),
            pltpu.VMEM((PAGE_SIZE, w), F32),
            pltpu.VMEM((PAGE_SIZE, w), F32),
        ],
    )
    return pl.pallas_call(
        functools.partial(_sb_sample_kernel, pages_per_step=g, t_new=t_new),
        grid_spec=grid_spec,
        out_shape=jax.ShapeDtypeStruct((b, rows, w), F32),
        compiler_params=_cparams("parallel", "arbitrary"),
        name="sb_sample",
    )(page_table_flat, qbd, bias_rows, k_new, v_new, *([cache_k] * g), *([cache_v] * g))


def _lower_bound(lb_logits, layer):
    m = jnp.max(lb_logits, axis=0, keepdims=True)
    e = jnp.exp(lb_logits - m)
    return jnp.sum(e[0:layer + 1], axis=0, keepdims=True) / jnp.sum(e, axis=0, keepdims=True)


def _hg_out(o, gain, gb):
    ms = jnp.mean(o * o, axis=-1, keepdims=True)
    return (o * lax.rsqrt(ms + EPS) * gain) * (gb * _sigmoid(gb))


def _hgrn_prompt_kernel(lbl_ref, gain_ref, hg_ref, o_ref, s_ref, st_ref, *, layer, n_chunks):
    c = HG_CHUNK
    w = HG_WIDTH
    half = HG_SUB // 2
    heads = range(HG_HEADS)
    t_id = pl.program_id(1)

    @pl.when(t_id == 0)
    def _():
        st_ref[...] = jnp.zeros_like(st_ref)

    lb_all = _lower_bound(lbl_ref[...], layer)
    gain = gain_ref[...]
    ti = lax.broadcasted_iota(jnp.int32, (c, c), 0)
    si = lax.broadcasted_iota(jnp.int32, (c, c), 1)
    ltri = (ti >= si).astype(BF16)
    ltri3 = jnp.concatenate([ltri, ltri, ltri], axis=1)
    rowc = lax.broadcasted_iota(jnp.int32, (c, HG_DK), 0)
    rowh = lax.broadcasted_iota(jnp.int32, (half, HG_DK), 0)

    def direct(g_q, q_q, g_k, k_k, v_k, causal):
        od = jnp.zeros((half, HG_DV), F32)
        for s in range(half):
            d = g_q - g_k[s:s + 1, :]
            if causal:
                d = jnp.where(rowh >= s, d, -jnp.inf)
            a = jnp.sum(q_q * (k_k[s:s + 1, :] * jnp.exp(d)), axis=1, keepdims=True)
            od = od + a * v_k[s:s + 1, :]
        return od

    def chunk(n, _):
        sl = pl.ds(pl.multiple_of(n * c, c), c)

        def col(which, h):
            lo = which * w + h * HG_DK
            return hg_ref[0, sl, lo:lo + HG_DK]

        f = [lb_all[:, h * HG_DK:(h + 1) * HG_DK]
             + (1.0 - lb_all[:, h * HG_DK:(h + 1) * HG_DK]) * _sigmoid(col(1, h)) for h in heads]
        kk = [1.0 - f[h] for h in heads]
        qq = [col(0, h) * _sigmoid(col(0, h)) for h in heads]
        v = [col(2, h) for h in heads]
        vb = [v[h].astype(BF16) for h in heads]
        gcum = [jnp.dot(ltri3, jnp.concatenate(_split3(jnp.log(f[h])), axis=0),
                        preferred_element_type=F32) for h in heads]
        g_last = [gcum[h][c - 1:c, :] for h in heads]
        st = [st_ref[h] for h in heads]
        o = [lax.dot_general((qq[h] * jnp.exp(gcum[h])).astype(BF16), st[h].astype(BF16),
                             NT_DIMS, preferred_element_type=F32) for h in heads]
        parts = [[jnp.zeros((HG_SUB, HG_DV), F32)] for _ in heads]
        for i in range(1, c // HG_SUB):
            lo = i * HG_SUB
            att = []
            for h in heads:
                r = gcum[h][lo - 1:lo, :]
                qh = qq[h][lo:lo + HG_SUB] * jnp.exp(gcum[h][lo:lo + HG_SUB] - r)
                kh = kk[h] * jnp.exp(jnp.where(rowc < lo, r - gcum[h], -jnp.inf))
                att.append(lax.dot_general(qh.astype(BF16), kh.astype(BF16), NT_DIMS,
                                           preferred_element_type=F32))
            for h in heads:
                parts[h].append(jnp.dot(att[h].astype(BF16), vb[h], preferred_element_type=F32))
        for h in heads:
            rows_out = []
            for i in range(c // HG_SUB):
                top = slice(i * HG_SUB, i * HG_SUB + half)
                bot = slice(i * HG_SUB + half, (i + 1) * HG_SUB)
                g, q, k, vv = gcum[h], qq[h], kk[h], v[h]
                od_top = direct(g[top], q[top], g[top], k[top], vv[top], True)
                od_bot = (direct(g[bot], q[bot], g[top], k[top], vv[top], False)
                          + direct(g[bot], q[bot], g[bot], k[bot], vv[bot], True))
                rows_out.append(jnp.concatenate([od_top, od_bot], axis=0) + parts[h][i])
            oh = o[h] + jnp.concatenate(rows_out, axis=0)
            o_ref[0, sl, h * HG_DV:(h + 1) * HG_DV] = _hg_out(oh, gain, col(3, h))
        for h in heads:
            ke = kk[h] * jnp.exp(g_last[h] - gcum[h])
            st_ref[h] = st[h] * jnp.exp(g_last[h]) + lax.dot_general(
                vb[h], ke.astype(BF16), TN_DIMS, preferred_element_type=F32)
        return 0

    lax.fori_loop(0, n_chunks, chunk, 0)

    @pl.when(t_id == pl.num_programs(1) - 1)
    def _():
        for h in heads:
            s_ref[0, h] = st_ref[h].T


def _hgrn_prompt(hg, lb_logits, gain, layer, tt):
    b, t, e = hg.shape
    nh = HG_HEADS
    return pl.pallas_call(
        functools.partial(_hgrn_prompt_kernel, layer=layer, n_chunks=tt // HG_CHUNK),
        grid=(b, t // tt),
        in_specs=[
            pl.BlockSpec(lb_logits.shape, lambda bi, ti: (0, 0)),
            pl.BlockSpec((1, HG_DV), lambda bi, ti: (0, 0)),
            pl.BlockSpec((1, tt, e), lambda bi, ti: (bi, ti, 0)),
        ],
        out_specs=[
            pl.BlockSpec((1, tt, HG_WIDTH), lambda bi, ti: (bi, ti, 0)),
            pl.BlockSpec((1, nh, HG_DK, HG_DV), lambda bi, ti: (bi, 0, 0, 0)),
        ],
        out_shape=[
            jax.ShapeDtypeStruct((b, t, HG_WIDTH), F32),
            jax.ShapeDtypeStruct((b, nh, HG_DK, HG_DV), F32),
        ],
        scratch_shapes=[pltpu.VMEM((nh, HG_DV, HG_DK), F32)],
        compiler_params=_cparams("parallel", "arbitrary"),
        name="hgrn_prompt",
    )(lb_logits, gain.reshape(1, HG_DV), hg)


def _hgrn_sample_kernel(lbl_ref, gain_ref, hg_ref, s0_ref, o_ref, s_ref, *, layer, t_new):
    w = HG_WIDTH
    mxu_rows = 16
    lb_all = _lower_bound(lbl_ref[...], layer)
    gain = gain_ref[...]
    x = hg_ref[0]
    rowt = lax.broadcasted_iota(jnp.int32, (t_new, HG_DK), 0)

    def padded(a):
        return jnp.concatenate([a, jnp.zeros((mxu_rows - t_new, a.shape[1]), F32)], axis=0)

    outs = []
    for h in range(HG_HEADS):
        c0 = h * HG_DK
        lb = lb_all[:, c0:c0 + HG_DK]
        qx = x[:, c0:c0 + HG_DK]
        qq = qx * _sigmoid(qx)
        f = lb + (1.0 - lb) * _sigmoid(x[:, w + c0:w + c0 + HG_DK])
        kk = 1.0 - f
        v = x[:, 2 * w + c0:2 * w + c0 + HG_DV]
        gb = x[:, 3 * w + c0:3 * w + c0 + HG_DV]
        logf = jnp.log(f)
        g_rows = [logf[0:1, :]]
        for t in range(1, t_new):
            g_rows.append(g_rows[-1] + logf[t:t + 1, :])
        gcum = jnp.concatenate(g_rows, axis=0)
        g_last = g_rows[-1]
        s0 = s0_ref[0, h]
        o = jnp.dot(padded(qq * jnp.exp(gcum)).astype(BF16), s0.astype(BF16),
                    preferred_element_type=F32)[0:t_new]
        for s in range(t_new):
            d = jnp.where(rowt >= s, gcum - g_rows[s], -jnp.inf)
            a = jnp.sum(qq * (kk[s:s + 1, :] * jnp.exp(d)), axis=1, keepdims=True)
            o = o + a * v[s:s + 1, :]
        ke = padded(kk * jnp.exp(g_last - gcum)).astype(BF16)
        outer = lax.dot_general(ke, padded(v).astype(BF16), TN_DIMS, preferred_element_type=F32)
        e_rows = jnp.concatenate([jnp.exp(g_last), jnp.zeros((HG_DK - 1, HG_DK), F32)], axis=0)
        s_ref[0, h] = e_rows.T[:, 0:1] * s0 + outer
        outs.append(_hg_out(o, gain, gb))
    o_ref[0] = jnp.concatenate(outs, axis=1)


def _hgrn_sample(hg, state, lb_logits, gain, layer):
    b, t_new, e = hg.shape
    nh = HG_HEADS
    return pl.pallas_call(
        functools.partial(_hgrn_sample_kernel, layer=layer, t_new=t_new),
        grid=(b,),
        in_specs=[
            pl.BlockSpec(lb_logits.shape, lambda bi: (0, 0)),
            pl.BlockSpec((1, HG_DV), lambda bi: (0, 0)),
            pl.BlockSpec((1, t_new, e), lambda bi: (bi, 0, 0)),
            pl.BlockSpec((1, nh, HG_DK, HG_DV), lambda bi: (bi, 0, 0, 0)),
        ],
        out_specs=[
            pl.BlockSpec((1, t_new, HG_WIDTH), lambda bi: (bi, 0, 0)),
            pl.BlockSpec((1, nh, HG_DK, HG_DV), lambda bi: (bi, 0, 0, 0)),
        ],
        out_shape=[
            jax.ShapeDtypeStruct((b, t_new, HG_WIDTH), F32),
            jax.ShapeDtypeStruct((b, nh, HG_DK, HG_DV), F32),
        ],
        compiler_params=_cparams("parallel"),
        name="hgrn_sample",
    )(lb_logits, gain.reshape(1, HG_DV), hg, state)


def _route(logits):
    lane = lax.broadcasted_iota(jnp.int32, logits.shape, 1)
    neg = -jnp.inf
    big = ROUTER_LANES

    def top(mask):
        val = jnp.max(jnp.where(mask, logits, neg), axis=1, keepdims=True)
        idx = jnp.min(jnp.where(mask & (logits == val), lane, big), axis=1, keepdims=True)
        return val, idx

    is_group = lane < N_GROUPS
    gmax, gidx = top(is_group)
    p_top = 1.0 / jnp.sum(jnp.where(is_group, jnp.exp(logits - gmax), 0.0), axis=1, keepdims=True)
    lo = N_GROUPS + gidx * EXPERTS_PER_GROUP
    in_group = (lane >= lo) & (lane < lo + EXPERTS_PER_GROUP)
    v1, i1 = top(in_group)
    v2, i2 = top(in_group & (lane != i1))
    e2 = jnp.exp(v2 - v1)
    w1 = p_top / (1.0 + e2)
    w2 = p_top * e2 / (1.0 + e2)
    return jnp.where(lane == i1, w1, 0.0) + jnp.where(lane == i2, w2, 0.0)


def _post_kernel(x_ref, oa_ref, ob_ref, sbg_ref, wo_ref, ng_ref, wr_ref, br_ref,
                 x1_ref, h2_ref, gate_ref, *, oa_transposed):
    w = SB_WIDTH
    if oa_transposed:
        oa = jnp.concatenate([oa_ref[0, hh] for hh in range(SB_HEADS)], axis=0)
        ms = jnp.mean(oa * oa, axis=0, keepdims=True)
        oa = (oa * lax.rsqrt(ms + EPS) * sbg_ref[...]).astype(BF16)
        mixed = lax.dot_general(oa, wo_ref[0:w, :], TN_DIMS, preferred_element_type=F32)
    else:
        oa = oa_ref[...]
        ms = jnp.mean(oa * oa, axis=-1, keepdims=True)
        oa = (oa * lax.rsqrt(ms + EPS) * sbg_ref[...]).astype(BF16)
        mixed = jnp.dot(oa, wo_ref[0:w, :], preferred_element_type=F32)
    mixed = mixed + jnp.dot(ob_ref[...].astype(BF16), wo_ref[w:, :], preferred_element_type=F32)
    x1 = x_ref[...] + mixed
    x1_ref[...] = x1
    ms = jnp.mean(x1 * x1, axis=-1, keepdims=True)
    h2 = x1 * lax.rsqrt(ms + EPS) * ng_ref[...]
    h2_ref[...] = h2.astype(BF16)
    hh, hl = _split2(h2)
    logits = (jnp.dot(hh, wr_ref[0], preferred_element_type=F32)
              + jnp.dot(hl, wr_ref[0], preferred_element_type=F32)
              + jnp.dot(hh, wr_ref[1], preferred_element_type=F32)) + br_ref[...]
    gate_ref[...] = _route(logits)


def _post(x2d, oa, ob, sb_gain, w_out_bf16, ffn_gain, w_router_hl, b_router, tm):
    n, d = x2d.shape
    row = lambda i: (i, 0)
    fixed = lambda i: (0, 0)
    oa_transposed = oa.ndim == 4
    if oa_transposed:
        per_b = oa.shape[3] // tm
        oa_spec = pl.BlockSpec((1, SB_HEADS, SB_HEAD_DIM, tm),
                               lambda i: (i // per_b, 0, 0, i % per_b))
        sb_gain = sb_gain.reshape(SB_WIDTH, 1)
    else:
        oa_spec = pl.BlockSpec((tm, SB_WIDTH), row)
        sb_gain = sb_gain.reshape(1, SB_WIDTH)
    return pl.pallas_call(
        functools.partial(_post_kernel, oa_transposed=oa_transposed),
        grid=(n // tm,),
        in_specs=[
            pl.BlockSpec((tm, d), row),
            oa_spec,
            pl.BlockSpec((tm, HG_WIDTH), row),
            pl.BlockSpec(sb_gain.shape, fixed),
            pl.BlockSpec(w_out_bf16.shape, fixed),
            pl.BlockSpec((1, d), fixed),
            pl.BlockSpec(w_router_hl.shape, lambda i: (0, 0, 0)),
            pl.BlockSpec((1, ROUTER_LANES), fixed),
        ],
        out_specs=[
            pl.BlockSpec((tm, d), row),
            pl.BlockSpec((tm, d), row),
            pl.BlockSpec((tm, ROUTER_LANES), row),
        ],
        out_shape=[
            jax.ShapeDtypeStruct((n, d), F32),
            jax.ShapeDtypeStruct((n, d), BF16),
            jax.ShapeDtypeStruct((n, ROUTER_LANES), F32),
        ],
        compiler_params=_cparams("parallel"),
        name="post",
    )(x2d, oa, ob, sb_gain, w_out_bf16, ffn_gain.reshape(1, d), w_router_hl, b_router)


def _moe_kernel(x1_ref, h2_ref, gate_ref, wg_ref, wu_ref, wd_ref, fg_ref, y_ref, acc_ref):
    e = pl.program_id(1)

    @pl.when(e == 0)
    def _():
        acc_ref[...] = jnp.zeros_like(acc_ref)

    h2 = h2_ref[...]
    gate = gate_ref[...]
    lane = lax.broadcasted_iota(jnp.int32, gate.shape, 1)
    gcol = jnp.sum(jnp.where(lane == N_GROUPS + e, gate, 0.0), axis=1, keepdims=True)
    a = jnp.dot(h2, wg_ref[0], preferred_element_type=F32)
    u = jnp.dot(h2, wu_ref[0], preferred_element_type=F32)
    act = (a * _sigmoid(a)) * u * gcol
    acc_ref[...] += jnp.dot(act.astype(BF16), wd_ref[0], preferred_element_type=F32)

    @pl.when(e == pl.num_programs(1) - 1)
    def _():
        x2 = x1_ref[...] + acc_ref[...]
        ms = jnp.mean(x2 * x2, axis=-1, keepdims=True)
        y_ref[...] = x2 * lax.rsqrt(ms + EPS) * fg_ref[...]


def _moe(x1, h2, gate, wg, wu, wd, final_gain, tm):
    n, d = x1.shape
    ne, _, df = wg.shape
    row = lambda i, e: (i, 0)
    return pl.pallas_call(
        _moe_kernel,
        grid=(n // tm, ne),
        in_specs=[
            pl.BlockSpec((tm, d), row),
            pl.BlockSpec((tm, d), row),
            pl.BlockSpec((tm, ROUTER_LANES), row),
            pl.BlockSpec((1, d, df), lambda i, e: (e, 0, 0)),
            pl.BlockSpec((1, d, df), lambda i, e: (e, 0, 0)),
            pl.BlockSpec((1, df, d), lambda i, e: (e, 0, 0)),
            pl.BlockSpec((1, d), lambda i, e: (0, 0)),
        ],
        out_specs=pl.BlockSpec((tm, d), row),
        out_shape=jax.ShapeDtypeStruct((n, d), F32),
        scratch_shapes=[pltpu.VMEM((tm, d), F32)],
        compiler_params=_cparams("parallel", "arbitrary"),
        name="moe",
    )(x1, h2, gate, wg, wu, wd, final_gain.reshape(1, d))


def _tile(n, pref):
    return pref if n % pref == 0 else n


def _router_params(w_rg, b_rg, w_re, b_re):
    d = w_rg.shape[0]
    pad = ROUTER_LANES - N_GROUPS - N_EXPERTS
    w = jnp.concatenate([w_rg, w_re, jnp.zeros((d, pad), F32)], axis=1)
    hi = w.astype(BF16)
    lo = (w - hi.astype(F32)).astype(BF16)
    b = jnp.concatenate([b_rg, b_re, jnp.zeros((pad,), F32)]).reshape(1, ROUTER_LANES)
    return jnp.stack([hi, lo]), b


def _ffn(x2d, oa, ob, sb_gain, w_out_bf16, ffn_gain, w_router_hl, b_router, wg, wu, wd, final_gain):
    n = x2d.shape[0]
    x1, h2, gate = _post(x2d, oa, ob, sb_gain, w_out_bf16, ffn_gain, w_router_hl, b_router,
                         _tile(n, 512))
    return _moe(x1, h2, gate, wg, wu, wd, final_gain, _tile(n, 1024))


def kernel(x_prompt, x_sample, cache_k, cache_v, state_hgrn, page_table, norm_attn, w_in,
           sb_logit_bias, sb_norm, hg_lb_logits, hg_norm, w_out, norm_ffn, w_router_group,
           b_router_group, w_router_expert, b_router_expert, w_exp_gate, w_exp_up, w_exp_down,
           norm_final):
    depth = w_in.shape[0]
    assert depth == 1, "single-layer trunk"
    l = 0
    bp, t, d = x_prompt.shape
    db, tn, _ = x_sample.shape
    nh, hd = SB_HEADS, SB_HEAD_DIM
    n_pages = page_table.shape[1]

    w_in_b = w_in[l].astype(BF16)
    w_out_b = w_out[l].astype(BF16)
    wr_hl, b_r = _router_params(w_router_group[l], b_router_group[l], w_router_expert[l],
                                b_router_expert[l])
    df = w_exp_gate.shape[-1]
    wg = w_exp_gate[l].reshape(N_EXPERTS, d, df).astype(BF16)
    wu = w_exp_up[l].reshape(N_EXPERTS, d, df).astype(BF16)
    wd = w_exp_down[l].reshape(N_EXPERTS, df, d).astype(BF16)
    bias = sb_logit_bias[l].astype(F32)

    xp = x_prompt.reshape(bp * t, d)
    w3 = 3 * SB_WIDTH
    wq_scaled = w_in[l][:, :SB_WIDTH] * SB_SCALE
    wqkv_t = jnp.concatenate([wq_scaled, w_in[l][:, SB_WIDTH:w3]], axis=1).T.astype(BF16)
    ktp, vtp, qtb, kb, vtb, hgp = _inproj_prompt(
        x_prompt, norm_attn[l], wqkv_t, w_in_b[:, SB_WIDTH:2 * SB_WIDTH], w_in_b[:, w3:],
        SB_QUERY_BLOCK, SB_QUERY_BLOCK)
    oatp = _sb_prompt(qtb, kb, vtb, bias, SB_KEY_BLOCK)
    obp, sp = _hgrn_prompt(hgp, hg_lb_logits, hg_norm[l], l, 512)
    yp = _ffn(xp, oatp, obp.reshape(bp * t, HG_WIDTH), sb_norm[l], w_out_b, norm_ffn[l], wr_hl,
              b_r, wg, wu, wd, norm_final)
    kp = ktp.transpose(0, 3, 1, 2)[None]
    vp = vtp.transpose(0, 3, 1, 2)[None]

    xs = x_sample.reshape(db * tn, d)
    qs, ks, vs, hgs = _inproj(xs, norm_attn[l], w_in_b, 256)
    head_of_lane = jnp.arange(SB_WIDTH) // hd
    onehot = (head_of_lane[None, :] == jnp.arange(nh)[:, None]).astype(F32)
    qbd = (qs.reshape(db, 1, tn, SB_WIDTH) * SB_SCALE) * onehot[None, :, None, :]
    qbd = qbd.reshape(db, nh * tn, SB_WIDTH).astype(BF16)
    bias_rows = jnp.broadcast_to(jnp.repeat(bias, tn)[:, None], (nh * tn, PAGE_SIZE))
    pad = ((0, 0), (0, 8 - tn), (0, 0))
    kn = jnp.pad(ks.reshape(db, tn, SB_WIDTH), pad)
    vn = jnp.pad(vs.reshape(db, tn, SB_WIDTH), pad)
    ck = cache_k[l].transpose(0, 2, 3, 1).reshape(-1, SB_WIDTH, PAGE_SIZE)
    cv = cache_v[l].transpose(0, 2, 3, 1).reshape(-1, SB_WIDTH, PAGE_SIZE)
    o_full = _sb_sample(qbd, bias_rows, kn, vn, ck, cv, page_table.reshape(-1), n_pages, tn, 16)
    oas = jnp.sum(o_full.reshape(db, nh, tn, SB_WIDTH) * onehot[None, :, None, :], axis=1)
    obs, ss = _hgrn_sample(hgs.reshape(db, tn, -1), state_hgrn[l], hg_lb_logits, hg_norm[l], l)
    ys = _ffn(xs, oas.reshape(db * tn, SB_WIDTH), obs.reshape(db * tn, HG_WIDTH), sb_norm[l],
              w_out_b, norm_ffn[l], wr_hl, b_r, wg, wu, wd, norm_final)

    sdt = state_hgrn.dtype
    return (yp.reshape(bp, t, d), ys.reshape(db, tn, d), kp, vp, sp.astype(sdt)[None],
            ks.reshape(1, db, tn, nh, hd), vs.reshape(1, db, tn, nh, hd), ss.astype(sdt)[None])
```

```python
import functools

import jax
import jax.numpy as jnp
from jax import lax
from jax.experimental import pallas as pl
from jax.experimental.pallas import tpu as pltpu

F32 = jnp.float32
BF16 = jnp.bfloat16
EPS = 1e-6
NEG_LOG2E = -1.4426950408889634

SB_HEADS = 8
SB_HEAD_DIM = 64
SB_WIDTH = SB_HEADS * SB_HEAD_DIM
SB_SCALE = SB_HEAD_DIM ** -0.5
HG_HEADS = 4
HG_DK = 128
HG_DV = 128
HG_WIDTH = HG_HEADS * HG_DK
HG_CHUNK = 64
HG_SUB = 16
N_GROUPS = 4
EXPERTS_PER_GROUP = 8
N_EXPERTS = N_GROUPS * EXPERTS_PER_GROUP
ROUTER_LANES = 128
GROUP_ID_LANE = ROUTER_LANES - 1
MOE_ROW_BLOCK = 128
MOE_SORT_TILE = 1024
PAGE_SIZE = 128
SB_KEY_BLOCK = 128
SB_QUERY_BLOCK = 512

VMEM_LIMIT_BYTES = 56 * 1024 * 1024

NT_DIMS = (((1,), (1,)), ((), ()))
TN_DIMS = (((0,), (0,)), ((), ()))


def _cparams(*sem):
    return pltpu.CompilerParams(dimension_semantics=sem, vmem_limit_bytes=VMEM_LIMIT_BYTES)


def _split2(x):
    hi = x.astype(BF16)
    lo = (x - hi.astype(F32)).astype(BF16)
    return hi, lo


def _split3(x):
    hi = x.astype(BF16)
    r = x - hi.astype(F32)
    mid = r.astype(BF16)
    lo = (r - mid.astype(F32)).astype(BF16)
    return hi, mid, lo


def _softplus(z):
    return jnp.maximum(z, 0.0) + jnp.log(1.0 + jnp.exp2(jnp.abs(z) * NEG_LOG2E))


def _sigmoid(x):
    return 1.0 / (1.0 + jnp.exp(-x))


def _inproj_kernel(x_ref, g_ref, w_ref, q_ref, k_ref, v_ref, hg_ref):
    x = x_ref[...]
    ms = jnp.mean(x * x, axis=-1, keepdims=True)
    h = (x * lax.rsqrt(ms + EPS) * g_ref[...]).astype(BF16)
    w = SB_WIDTH
    q_ref[...] = jnp.dot(h, w_ref[:, 0:w], preferred_element_type=F32)
    k_ref[...] = jnp.dot(h, w_ref[:, w:2 * w], preferred_element_type=F32)
    v_ref[...] = jnp.dot(h, w_ref[:, 2 * w:3 * w], preferred_element_type=F32)
    hg_ref[...] = jnp.dot(h, w_ref[:, 3 * w:], preferred_element_type=F32)


def _inproj(x2d, gain, w_in_bf16, tm):
    n, d = x2d.shape
    e = w_in_bf16.shape[1]
    hgw = e - 3 * SB_WIDTH
    return pl.pallas_call(
        _inproj_kernel,
        grid=(n // tm,),
        in_specs=[
            pl.BlockSpec((tm, d), lambda i: (i, 0)),
            pl.BlockSpec((1, d), lambda i: (0, 0)),
            pl.BlockSpec((d, e), lambda i: (0, 0)),
        ],
        out_specs=[
            pl.BlockSpec((tm, SB_WIDTH), lambda i: (i, 0)),
            pl.BlockSpec((tm, SB_WIDTH), lambda i: (i, 0)),
            pl.BlockSpec((tm, SB_WIDTH), lambda i: (i, 0)),
            pl.BlockSpec((tm, hgw), lambda i: (i, 0)),
        ],
        out_shape=[
            jax.ShapeDtypeStruct((n, SB_WIDTH), F32),
            jax.ShapeDtypeStruct((n, SB_WIDTH), F32),
            jax.ShapeDtypeStruct((n, SB_WIDTH), F32),
            jax.ShapeDtypeStruct((n, hgw), F32),
        ],
        compiler_params=_cparams("parallel"),
        name="inproj",
    )(x2d, gain.reshape(1, d), w_in_bf16)


def _inproj_prompt_kernel(x_ref, g_ref, wt_ref, wk_ref, whg_ref,
                          kt_ref, vt_ref, qtb_ref, kb_ref, vtb_ref, hg_ref, *, tk):
    x = x_ref[0]
    tm = x.shape[0]
    ms = jnp.mean(x * x, axis=-1, keepdims=True)
    h = (x * lax.rsqrt(ms + EPS) * g_ref[...]).astype(BF16)
    qkvt = lax.dot_general(wt_ref[...], h, NT_DIMS, preferred_element_type=F32)
    k = jnp.dot(h, wk_ref[...], preferred_element_type=F32)
    w, hd = SB_WIDTH, SB_HEAD_DIM
    for hh in range(SB_HEADS):
        r = hh * hd
        qtb_ref[0, hh] = qkvt[r:r + hd].astype(BF16)
        kt_ref[0, hh] = qkvt[w + r:w + r + hd]
        vt = qkvt[2 * w + r:2 * w + r + hd]
        vt_ref[0, hh] = vt
        for c in range(tm // tk):
            vtb_ref[0, hh, c] = vt[:, c * tk:(c + 1) * tk].astype(BF16)
        kb_ref[0, hh] = k[:, r:r + hd].astype(BF16)
    hg_ref[0] = jnp.dot(h, whg_ref[...], preferred_element_type=F32)


def _inproj_prompt(x, gain, wqkv_t, wk, whg, tm, tk):
    b, t, d = x.shape
    nh, hd, w = SB_HEADS, SB_HEAD_DIM, SB_WIDTH
    hgw = whg.shape[1]
    fixed = lambda bi, i: (0, 0)
    tspec = pl.BlockSpec((1, nh, hd, tm), lambda bi, i: (bi, 0, 0, i))
    return pl.pallas_call(
        functools.partial(_inproj_prompt_kernel, tk=tk),
        grid=(b, t // tm),
        in_specs=[
            pl.BlockSpec((1, tm, d), lambda bi, i: (bi, i, 0)),
            pl.BlockSpec((1, d), fixed),
            pl.BlockSpec((3 * w, d), fixed),
            pl.BlockSpec((d, w), fixed),
            pl.BlockSpec((d, hgw), fixed),
        ],
        out_specs=[
            tspec, tspec, tspec,
            pl.BlockSpec((1, nh, tm, hd), lambda bi, i: (bi, 0, i, 0)),
            pl.BlockSpec((1, nh, tm // tk, hd, tk), lambda bi, i: (bi, 0, i, 0, 0)),
            pl.BlockSpec((1, tm, hgw), lambda bi, i: (bi, i, 0)),
        ],
        out_shape=[
            jax.ShapeDtypeStruct((b, nh, hd, t), F32),
            jax.ShapeDtypeStruct((b, nh, hd, t), F32),
            jax.ShapeDtypeStruct((b, nh, hd, t), BF16),
            jax.ShapeDtypeStruct((b, nh, t, hd), BF16),
            jax.ShapeDtypeStruct((b, nh, t // tk, hd, tk), BF16),
            jax.ShapeDtypeStruct((b, t, hgw), F32),
        ],
        compiler_params=_cparams("parallel", "parallel"),
        name="inproj_prompt",
    )(x, gain.reshape(1, d), wqkv_t, wk, whg)


def _sb_row_weights(z, tri2, carry, n, before=None):
    sp = _softplus(z)
    if before is not None:
        sp = jnp.where(before, sp, 0.0)
        z = jnp.where(before, z, -jnp.inf)
    hi, lo = _split2(sp)
    blocks = []
    for p in range(z.shape[1] // n):
        sl = slice(p * n, (p + 1) * n)
        cs = jnp.dot(jnp.concatenate([hi[:, sl], lo[:, sl]], axis=1), tri2,
                     preferred_element_type=F32)
        blocks.append(jnp.exp(z[:, sl] - cs - carry).astype(BF16))
        carry = carry + cs[:, 0:1]
    return jnp.concatenate(blocks, axis=1), carry


def _tri_incl(n):
    j = lax.broadcasted_iota(jnp.int32, (n, n), 0)
    s = lax.broadcasted_iota(jnp.int32, (n, n), 1)
    return (j >= s).astype(BF16)


def _sb_prompt_kernel(bias_ref, qt_ref, k_ref, vt_ref, ot_ref, z0_ref, hl0_ref, z1_ref, hl1_ref,
                      *, tq, tk):
    h = pl.program_id(1)
    i = pl.program_id(2)
    bias = bias_ref[h]
    qt = qt_ref[0, 0]
    nb = tq // tk
    r = lax.broadcasted_iota(jnp.int32, (tk, tk), 0)
    c = lax.broadcasted_iota(jnp.int32, (tk, tk), 1)
    triu = (c >= r).astype(BF16)
    tri2 = jnp.concatenate([triu, triu], axis=1)
    bufs = ((z0_ref, hl0_ref), (z1_ref, hl1_ref))

    def scores(g, slot, masked):
        z_ref, hl_ref = bufs[slot]
        rows = k_ref[0, 0, pl.ds(pl.multiple_of(g * tq, tq), tq), :]
        z = jnp.dot(rows, qt, preferred_element_type=F32) + bias
        sp = _softplus(z)
        if masked:
            s_loc = lax.broadcasted_iota(jnp.int32, (tq, tq), 0)
            t_loc = lax.broadcasted_iota(jnp.int32, (tq, tq), 1)
            before = s_loc < t_loc
            sp = jnp.where(before, sp, 0.0)
            z = jnp.where(before, z, -jnp.inf)
        z_ref[...] = z
        hi, lo = _split2(sp)
        for p in range(nb):
            hl_ref[p, 0:tk, :] = hi[p * tk:(p + 1) * tk]
            hl_ref[p, tk:2 * tk, :] = lo[p * tk:(p + 1) * tk]

    def weights(g, slot, carry, acc):
        z_ref, hl_ref = bufs[slot]
        a_blocks = [None] * nb
        for p in reversed(range(nb)):
            cs = jnp.dot(tri2, hl_ref[p], preferred_element_type=F32)
            la = z_ref[p * tk:(p + 1) * tk, :] - cs - carry
            a_blocks[p] = jnp.exp(la).astype(BF16)
            carry = carry + cs[0:1, :]
        a = jnp.concatenate(a_blocks, axis=0)
        return carry, acc + jnp.dot(vt_ref[0, 0, g], a, preferred_element_type=F32)

    scores(i, 0, True)

    def pair(m, c):
        g = i - 2 * m
        scores(g - 1, 1, False)
        c = weights(g, 0, *c)
        scores(g - 2, 0, False)
        return weights(g - 1, 1, *c)

    init = (jnp.zeros((1, tq), F32), jnp.zeros((SB_HEAD_DIM, tq), F32))
    c = lax.fori_loop(0, i // 2, pair, init)

    def odd_tail(c):
        scores(0, 1, False)
        return weights(0, 1, *weights(1, 0, *c))

    carry, acc = lax.cond(i % 2 == 1, odd_tail, lambda c: weights(0, 0, *c), c)
    ot_ref[0, 0] = acc


def _sb_prompt(qt, k, vt, bias, tk):
    b, h, d, t = qt.shape
    tq = vt.shape[-1]
    return pl.pallas_call(
        functools.partial(_sb_prompt_kernel, tq=tq, tk=tk),
        grid=(b, h, t // tq),
        in_specs=[
            pl.BlockSpec(memory_space=pltpu.SMEM),
            pl.BlockSpec((1, 1, d, tq), lambda bi, hi, i: (bi, hi, 0, i)),
            pl.BlockSpec((1, 1, t, d), lambda bi, hi, i: (bi, hi, 0, 0)),
            pl.BlockSpec((1, 1, t // tq, d, tq), lambda bi, hi, i: (bi, hi, 0, 0, 0)),
        ],
        out_specs=pl.BlockSpec((1, 1, d, tq), lambda bi, hi, i: (bi, hi, 0, i)),
        out_shape=jax.ShapeDtypeStruct((b, h, d, t), F32),
        scratch_shapes=[
            pltpu.VMEM((tq, tq), F32),
            pltpu.VMEM((tq // tk, 2 * tk, tq), BF16),
        ] * 2,
        compiler_params=_cparams("parallel", "parallel", "arbitrary"),
        name="sb_prompt",
    )(bias, qt, k, vt)


def _sb_sample_kernel(pt_ref, qbd_ref, bias_ref, kn_ref, vn_ref, *rest, pages_per_step, t_new):
    g = pages_per_step
    k_refs = rest[:g]
    v_refs = rest[g:2 * g]
    o_ref = rest[2 * g]
    carry_ref, acc_ref, kpad_ref, vpad_ref = rest[2 * g + 1:]
    step_id = pl.program_id(1)
    rows = qbd_ref.shape[1]
    n = PAGE_SIZE
    qbd = qbd_ref[0]
    bias = bias_ref[...]
    tri = _tri_incl(n)
    tri2 = jnp.concatenate([tri, tri], axis=0)

    @pl.when(step_id == 0)
    def _():
        kpad_ref[...] = jnp.zeros_like(kpad_ref)
        vpad_ref[...] = jnp.zeros_like(vpad_ref)
        kpad_ref[0:kn_ref.shape[1], :] = kn_ref[0]
        vpad_ref[0:vn_ref.shape[1], :] = vn_ref[0]
        qi = lax.broadcasted_iota(jnp.int32, (rows, n), 0) % t_new
        s = lax.broadcasted_iota(jnp.int32, (rows, n), 1)
        z = lax.dot_general(qbd, kpad_ref[...].astype(BF16), NT_DIMS,
                            preferred_element_type=F32) + bias
        a, carry = _sb_row_weights(z, tri2, jnp.zeros((rows, 1), F32), n, s < qi)
        carry_ref[...] = jnp.broadcast_to(carry, carry_ref.shape)
        acc_ref[...] = jnp.dot(a, vpad_ref[...].astype(BF16), preferred_element_type=F32)

    kcat = jnp.concatenate([k_refs[p][0].astype(BF16) for p in range(g)], axis=1)
    vcat = jnp.concatenate([v_refs[p][0].astype(BF16) for p in range(g)], axis=1)
    z = jnp.dot(qbd, kcat, preferred_element_type=F32) + jnp.tile(bias, (1, g))
    a, carry = _sb_row_weights(z, tri2, carry_ref[:, 0:1], n)
    acc = acc_ref[...] + lax.dot_general(a, vcat, NT_DIMS, preferred_element_type=F32)
    carry_ref[...] = jnp.broadcast_to(carry, carry_ref.shape)
    acc_ref[...] = acc

    @pl.when(step_id == pl.num_programs(1) - 1)
    def _():
        o_ref[0] = acc


def _sb_sample(qbd, bias_rows, k_new, v_new, cache_k, cache_v, page_table_flat, n_pages, t_new,
               pages_per_step):
    b, rows, w = qbd.shape
    g = pages_per_step
    steps = n_pages // g

    def page_map(p):
        return lambda bi, si, pt: (pt[bi * n_pages + (n_pages - 1 - (si * g + p))], 0, 0)

    page_specs = [pl.BlockSpec((1, w, PAGE_SIZE), page_map(p)) for p in range(g)]
    grid_spec = pltpu.PrefetchScalarGridSpec(
        num_scalar_prefetch=1,
        grid=(b, steps),
        in_specs=[
            pl.BlockSpec((1, rows, w), lambda bi, si, pt: (bi, 0, 0)),
            pl.BlockSpec((rows, PAGE_SIZE), lambda bi, si, pt: (0, 0)),
            pl.BlockSpec((1, k_new.shape[1], w), lambda bi, si, pt: (bi, 0, 0)),
            pl.BlockSpec((1, v_new.shape[1], w), lambda bi, si, pt: (bi, 0, 0)),
        ] + page_specs + page_specs,
        out_specs=pl.BlockSpec((1, rows, w), lambda bi, si, pt: (bi, 0, 0)),
        scratch_shapes=[
            pltpu.VMEM((rows, PAGE_SIZE), F32),
            pltpu.VMEM((rows, w), F32),
            pltpu.VMEM((PAGE_SIZE, w), F32),
            pltpu.VMEM((PAGE_SIZE, w), F32),
        ],
    )
    return pl.pallas_call(
        functools.partial(_sb_sample_kernel, pages_per_step=g, t_new=t_new),
        grid_spec=grid_spec,
        out_shape=jax.ShapeDtypeStruct((b, rows, w), F32),
        compiler_params=_cparams("parallel", "arbitrary"),
        name="sb_sample",
    )(page_table_flat, qbd, bias_rows, k_new, v_new, *([cache_k] * g), *([cache_v] * g))


def _lower_bound(lb_logits, layer):
    m = jnp.max(lb_logits, axis=0, keepdims=True)
    e = jnp.exp(lb_logits - m)
    return jnp.sum(e[0:layer + 1], axis=0, keepdims=True) / jnp.sum(e, axis=0, keepdims=True)


def _hg_out(o, gain, gb):
    ms = jnp.mean(o * o, axis=-1, keepdims=True)
    return (o * lax.rsqrt(ms + EPS) * gain) * (gb * _sigmoid(gb))


def _hgrn_prompt_kernel(lbl_ref, gain_ref, hg_ref, o_ref, s_ref, st_ref, *, layer, n_chunks):
    c = HG_CHUNK
    w = HG_WIDTH
    half = HG_SUB // 2
    heads = range(HG_HEADS)
    t_id = pl.program_id(1)

    @pl.when(t_id == 0)
    def _():
        st_ref[...] = jnp.zeros_like(st_ref)

    lb_all = _lower_bound(lbl_ref[...], layer)
    gain = gain_ref[...]
    ti = lax.broadcasted_iota(jnp.int32, (c, c), 0)
    si = lax.broadcasted_iota(jnp.int32, (c, c), 1)
    ltri = (ti >= si).astype(BF16)
    ltri3 = jnp.concatenate([ltri, ltri, ltri], axis=1)
    rowc = lax.broadcasted_iota(jnp.int32, (c, HG_DK), 0)
    rowh = lax.broadcasted_iota(jnp.int32, (half, HG_DK), 0)

    def direct(g_q, q_q, g_k, k_k, v_k, causal):
        od = jnp.zeros((half, HG_DV), F32)
        for s in range(half):
            d = g_q - g_k[s:s + 1, :]
            if causal:
                d = jnp.where(rowh >= s, d, -jnp.inf)
            a = jnp.sum(q_q * (k_k[s:s + 1, :] * jnp.exp(d)), axis=1, keepdims=True)
            od = od + a * v_k[s:s + 1, :]
        return od

    def chunk(n, _):
        sl = pl.ds(pl.multiple_of(n * c, c), c)

        def col(which, h):
            lo = which * w + h * HG_DK
            return hg_ref[0, sl, lo:lo + HG_DK]

        f = [lb_all[:, h * HG_DK:(h + 1) * HG_DK]
             + (1.0 - lb_all[:, h * HG_DK:(h + 1) * HG_DK]) * _sigmoid(col(1, h)) for h in heads]
        kk = [1.0 - f[h] for h in heads]
        qq = [col(0, h) * _sigmoid(col(0, h)) for h in heads]
        v = [col(2, h) for h in heads]
        vb = [v[h].astype(BF16) for h in heads]
        gcum = [jnp.dot(ltri3, jnp.concatenate(_split3(jnp.log(f[h])), axis=0),
                        preferred_element_type=F32) for h in heads]
        g_last = [gcum[h][c - 1:c, :] for h in heads]
        st = [st_ref[h] for h in heads]
        o = [lax.dot_general((qq[h] * jnp.exp(gcum[h])).astype(BF16), st[h].astype(BF16),
                             NT_DIMS, preferred_element_type=F32) for h in heads]
        parts = [[jnp.zeros((HG_SUB, HG_DV), F32)] for _ in heads]
        for i in range(1, c // HG_SUB):
            lo = i * HG_SUB
            att = []
            for h in heads:
                r = gcum[h][lo - 1:lo, :]
                qh = qq[h][lo:lo + HG_SUB] * jnp.exp(gcum[h][lo:lo + HG_SUB] - r)
                kh = kk[h] * jnp.exp(jnp.where(rowc < lo, r - gcum[h], -jnp.inf))
                att.append(lax.dot_general(qh.astype(BF16), kh.astype(BF16), NT_DIMS,
                                           preferred_element_type=F32))
            for h in heads:
                parts[h].append(jnp.dot(att[h].astype(BF16), vb[h], preferred_element_type=F32))
        for h in heads:
            rows_out = []
            for i in range(c // HG_SUB):
                top = slice(i * HG_SUB, i * HG_SUB + half)
                bot = slice(i * HG_SUB + half, (i + 1) * HG_SUB)
                g, q, k, vv = gcum[h], qq[h], kk[h], v[h]
                od_top = direct(g[top], q[top], g[top], k[top], vv[top], True)
                od_bot = (direct(g[bot], q[bot], g[top], k[top], vv[top], False)
                          + direct(g[bot], q[bot], g[bot], k[bot], vv[bot], True))
                rows_out.append(jnp.concatenate([od_top, od_bot], axis=0) + parts[h][i])
            oh = o[h] + jnp.concatenate(rows_out, axis=0)
            o_ref[0, sl, h * HG_DV:(h + 1) * HG_DV] = _hg_out(oh, gain, col(3, h))
        for h in heads:
            ke = kk[h] * jnp.exp(g_last[h] - gcum[h])
            st_ref[h] = st[h] * jnp.exp(g_last[h]) + lax.dot_general(
                vb[h], ke.astype(BF16), TN_DIMS, preferred_element_type=F32)
        return 0

    lax.fori_loop(0, n_chunks, chunk, 0)

    @pl.when(t_id == pl.num_programs(1) - 1)
    def _():
        for h in heads:
            s_ref[0, h] = st_ref[h].T


def _hgrn_prompt(hg, lb_logits, gain, layer, tt):
    b, t, e = hg.shape
    nh = HG_HEADS
    return pl.pallas_call(
        functools.partial(_hgrn_prompt_kernel, layer=layer, n_chunks=tt // HG_CHUNK),
        grid=(b, t // tt),
        in_specs=[
            pl.BlockSpec(lb_logits.shape, lambda bi, ti: (0, 0)),
            pl.BlockSpec((1, HG_DV), lambda bi, ti: (0, 0)),
            pl.BlockSpec((1, tt, e), lambda bi, ti: (bi, ti, 0)),
        ],
        out_specs=[
            pl.BlockSpec((1, tt, HG_WIDTH), lambda bi, ti: (bi, ti, 0)),
            pl.BlockSpec((1, nh, HG_DK, HG_DV), lambda bi, ti: (bi, 0, 0, 0)),
        ],
        out_shape=[
            jax.ShapeDtypeStruct((b, t, HG_WIDTH), F32),
            jax.ShapeDtypeStruct((b, nh, HG_DK, HG_DV), F32),
        ],
        scratch_shapes=[pltpu.VMEM((nh, HG_DV, HG_DK), F32)],
        compiler_params=_cparams("parallel", "arbitrary"),
        name="hgrn_prompt",
    )(lb_logits, gain.reshape(1, HG_DV), hg)


def _hgrn_sample_kernel(lbl_ref, gain_ref, hg_ref, s0_ref, o_ref, s_ref, *, layer, t_new):
    w = HG_WIDTH
    mxu_rows = 16
    lb_all = _lower_bound(lbl_ref[...], layer)
    gain = gain_ref[...]
    x = hg_ref[0]
    rowt = lax.broadcasted_iota(jnp.int32, (t_new, HG_DK), 0)

    def padded(a):
        return jnp.concatenate([a, jnp.zeros((mxu_rows - t_new, a.shape[1]), F32)], axis=0)

    outs = []
    for h in range(HG_HEADS):
        c0 = h * HG_DK
        lb = lb_all[:, c0:c0 + HG_DK]
        qx = x[:, c0:c0 + HG_DK]
        qq = qx * _sigmoid(qx)
        f = lb + (1.0 - lb) * _sigmoid(x[:, w + c0:w + c0 + HG_DK])
        kk = 1.0 - f
        v = x[:, 2 * w + c0:2 * w + c0 + HG_DV]
        gb = x[:, 3 * w + c0:3 * w + c0 + HG_DV]
        logf = jnp.log(f)
        g_rows = [logf[0:1, :]]
        for t in range(1, t_new):
            g_rows.append(g_rows[-1] + logf[t:t + 1, :])
        gcum = jnp.concatenate(g_rows, axis=0)
        g_last = g_rows[-1]
        s0 = s0_ref[0, h]
        o = jnp.dot(padded(qq * jnp.exp(gcum)).astype(BF16), s0.astype(BF16),
                    preferred_element_type=F32)[0:t_new]
        for s in range(t_new):
            d = jnp.where(rowt >= s, gcum - g_rows[s], -jnp.inf)
            a = jnp.sum(qq * (kk[s:s + 1, :] * jnp.exp(d)), axis=1, keepdims=True)
            o = o + a * v[s:s + 1, :]
        ke = padded(kk * jnp.exp(g_last - gcum)).astype(BF16)
        outer = lax.dot_general(ke, padded(v).astype(BF16), TN_DIMS, preferred_element_type=F32)
        e_rows = jnp.concatenate([jnp.exp(g_last), jnp.zeros((HG_DK - 1, HG_DK), F32)], axis=0)
        s_ref[0, h] = e_rows.T[:, 0:1] * s0 + outer
        outs.append(_hg_out(o, gain, gb))
    o_ref[0] = jnp.concatenate(outs, axis=1)


def _hgrn_sample(hg, state, lb_logits, gain, layer):
    b, t_new, e = hg.shape
    nh = HG_HEADS
    return pl.pallas_call(
        functools.partial(_hgrn_sample_kernel, layer=layer, t_new=t_new),
        grid=(b,),
        in_specs=[
            pl.BlockSpec(lb_logits.shape, lambda bi: (0, 0)),
            pl.BlockSpec((1, HG_DV), lambda bi: (0, 0)),
            pl.BlockSpec((1, t_new, e), lambda bi: (bi, 0, 0)),
            pl.BlockSpec((1, nh, HG_DK, HG_DV), lambda bi: (bi, 0, 0, 0)),
        ],
        out_specs=[
            pl.BlockSpec((1, t_new, HG_WIDTH), lambda bi: (bi, 0, 0)),
            pl.BlockSpec((1, nh, HG_DK, HG_DV), lambda bi: (bi, 0, 0, 0)),
        ],
        out_shape=[
            jax.ShapeDtypeStruct((b, t_new, HG_WIDTH), F32),
            jax.ShapeDtypeStruct((b, nh, HG_DK, HG_DV), F32),
        ],
        compiler_params=_cparams("parallel"),
        name="hgrn_sample",
    )(lb_logits, gain.reshape(1, HG_DV), hg, state)


def _route(logits):
    lane = lax.broadcasted_iota(jnp.int32, logits.shape, 1)
    neg = -jnp.inf
    big = ROUTER_LANES

    def top(mask):
        val = jnp.max(jnp.where(mask, logits, neg), axis=1, keepdims=True)
        idx = jnp.min(jnp.where(mask & (logits == val), lane, big), axis=1, keepdims=True)
        return val, idx

    is_group = lane < N_GROUPS
    gmax, gidx = top(is_group)
    p_top = 1.0 / jnp.sum(jnp.where(is_group, jnp.exp(logits - gmax), 0.0), axis=1, keepdims=True)
    lo = N_GROUPS + gidx * EXPERTS_PER_GROUP
    in_group = (lane >= lo) & (lane < lo + EXPERTS_PER_GROUP)
    v1, i1 = top(in_group)
    v2, i2 = top(in_group & (lane != i1))
    e2 = jnp.exp(v2 - v1)
    w1 = p_top / (1.0 + e2)
    w2 = p_top * e2 / (1.0 + e2)
    gate = jnp.where(lane == i1, w1, 0.0) + jnp.where(lane == i2, w2, 0.0)
    return jnp.where(lane == GROUP_ID_LANE, gidx.astype(F32), gate)


def _post_kernel(x_ref, oa_ref, ob_ref, sbg_ref, wo_ref, ng_ref, wr_ref, br_ref,
                 x1_ref, h2_ref, gate_ref, *, oa_transposed):
    w = SB_WIDTH
    if oa_transposed:
        oa = jnp.concatenate([oa_ref[0, hh] for hh in range(SB_HEADS)], axis=0)
        ms = jnp.mean(oa * oa, axis=0, keepdims=True)
        oa = (oa * lax.rsqrt(ms + EPS) * sbg_ref[...]).astype(BF16)
        mixed = lax.dot_general(oa, wo_ref[0:w, :], TN_DIMS, preferred_element_type=F32)
    else:
        oa = oa_ref[...]
        ms = jnp.mean(oa * oa, axis=-1, keepdims=True)
        oa = (oa * lax.rsqrt(ms + EPS) * sbg_ref[...]).astype(BF16)
        mixed = jnp.dot(oa, wo_ref[0:w, :], preferred_element_type=F32)
    mixed = mixed + jnp.dot(ob_ref[...].astype(BF16), wo_ref[w:, :], preferred_element_type=F32)
    x1 = x_ref[...] + mixed
    x1_ref[...] = x1
    ms = jnp.mean(x1 * x1, axis=-1, keepdims=True)
    h2 = x1 * lax.rsqrt(ms + EPS) * ng_ref[...]
    h2_ref[...] = h2.astype(BF16)
    hh, hl = _split2(h2)
    logits = (jnp.dot(hh, wr_ref[0], preferred_element_type=F32)
              + jnp.dot(hl, wr_ref[0], preferred_element_type=F32)
              + jnp.dot(hh, wr_ref[1], preferred_element_type=F32)) + br_ref[...]
    gate_ref[...] = _route(logits)


def _post(x2d, oa, ob, sb_gain, w_out_bf16, ffn_gain, w_router_hl, b_router, tm):
    n, d = x2d.shape
    row = lambda i: (i, 0)
    fixed = lambda i: (0, 0)
    oa_transposed = oa.ndim == 4
    if oa_transposed:
        per_b = oa.shape[3] // tm
        oa_spec = pl.BlockSpec((1, SB_HEADS, SB_HEAD_DIM, tm),
                               lambda i: (i // per_b, 0, 0, i % per_b))
        sb_gain = sb_gain.reshape(SB_WIDTH, 1)
    else:
        oa_spec = pl.BlockSpec((tm, SB_WIDTH), row)
        sb_gain = sb_gain.reshape(1, SB_WIDTH)
    return pl.pallas_call(
        functools.partial(_post_kernel, oa_transposed=oa_transposed),
        grid=(n // tm,),
        in_specs=[
            pl.BlockSpec((tm, d), row),
            oa_spec,
            pl.BlockSpec((tm, HG_WIDTH), row),
            pl.BlockSpec(sb_gain.shape, fixed),
            pl.BlockSpec(w_out_bf16.shape, fixed),
            pl.BlockSpec((1, d), fixed),
            pl.BlockSpec(w_router_hl.shape, lambda i: (0, 0, 0)),
            pl.BlockSpec((1, ROUTER_LANES), fixed),
        ],
        out_specs=[
            pl.BlockSpec((tm, d), row),
            pl.BlockSpec((tm, d), row),
            pl.BlockSpec((tm, ROUTER_LANES), row),
        ],
        out_shape=[
            jax.ShapeDtypeStruct((n, d), F32),
            jax.ShapeDtypeStruct((n, d), BF16),
            jax.ShapeDtypeStruct((n, ROUTER_LANES), F32),
        ],
        compiler_params=_cparams("parallel"),
        name="post",
    )(x2d, oa, ob, sb_gain, w_out_bf16, ffn_gain.reshape(1, d), w_router_hl, b_router)


def _moe_kernel(x1_ref, h2_ref, gate_ref, wg_ref, wu_ref, wd_ref, fg_ref, y_ref, acc_ref):
    e = pl.program_id(1)

    @pl.when(e == 0)
    def _():
        acc_ref[...] = jnp.zeros_like(acc_ref)

    h2 = h2_ref[...]
    gate = gate_ref[...]
    lane = lax.broadcasted_iota(jnp.int32, gate.shape, 1)
    gcol = jnp.sum(jnp.where(lane == N_GROUPS + e, gate, 0.0), axis=1, keepdims=True)
    a = jnp.dot(h2, wg_ref[0], preferred_element_type=F32)
    u = jnp.dot(h2, wu_ref[0], preferred_element_type=F32)
    act = (a * _sigmoid(a)) * u * gcol
    acc_ref[...] += jnp.dot(act.astype(BF16), wd_ref[0], preferred_element_type=F32)

    @pl.when(e == pl.num_programs(1) - 1)
    def _():
        x2 = x1_ref[...] + acc_ref[...]
        ms = jnp.mean(x2 * x2, axis=-1, keepdims=True)
        y_ref[...] = x2 * lax.rsqrt(ms + EPS) * fg_ref[...]


def _moe(x1, h2, gate, wg, wu, wd, final_gain, tm):
    n, d = x1.shape
    ne, _, df = wg.shape
    row = lambda i, e: (i, 0)
    return pl.pallas_call(
        _moe_kernel,
        grid=(n // tm, ne),
        in_specs=[
            pl.BlockSpec((tm, d), row),
            pl.BlockSpec((tm, d), row),
            pl.BlockSpec((tm, ROUTER_LANES), row),
            pl.BlockSpec((1, d, df), lambda i, e: (e, 0, 0)),
            pl.BlockSpec((1, d, df), lambda i, e: (e, 0, 0)),
            pl.BlockSpec((1, df, d), lambda i, e: (e, 0, 0)),
            pl.BlockSpec((1, d), lambda i, e: (0, 0)),
        ],
        out_specs=pl.BlockSpec((tm, d), row),
        out_shape=jax.ShapeDtypeStruct((n, d), F32),
        scratch_shapes=[pltpu.VMEM((tm, d), F32)],
        compiler_params=_cparams("parallel", "arbitrary"),
        name="moe",
    )(x1, h2, gate, wg, wu, wd, final_gain.reshape(1, d))


def _moe_sorted_kernel(x1_ref, h2_ref, gate_ref, wg_ref, wu_ref, wd_ref, fg_ref, y_ref,
                       pt_ref, p_ref, hs_ref, ys_ref, gs_ref, meta_ref, *, rb):
    e = pl.program_id(1)
    tm = h2_ref.shape[0]
    tmp = hs_ref.shape[0]
    chunk = 256

    @pl.when(e == 0)
    def _():
        gate = gate_ref[...]
        lane = lax.broadcasted_iota(jnp.int32, gate.shape, 1)
        onehot = lane.astype(F32) == gate[:, GROUP_ID_LANE:GROUP_ID_LANE + 1]
        r = lax.broadcasted_iota(jnp.int32, (tm, tm), 0)
        c = lax.broadcasted_iota(jnp.int32, (tm, tm), 1)
        earlier = jnp.dot((c < r).astype(BF16), onehot.astype(BF16),
                          preferred_element_type=F32)
        count = earlier[tm - 1:tm, :] + onehot[tm - 1:tm, :].astype(F32)
        padded = jnp.ceil(count * (1.0 / rb)) * rb
        g0 = lax.broadcasted_iota(jnp.int32, (ROUTER_LANES, ROUTER_LANES), 0)
        g1 = lax.broadcasted_iota(jnp.int32, (ROUTER_LANES, ROUTER_LANES), 1)
        start = jnp.dot(jnp.broadcast_to(padded, (8, ROUTER_LANES)).astype(BF16),
                        (g0 < g1).astype(BF16), preferred_element_type=F32)[0:1, :]
        pos = jnp.sum(jnp.where(onehot, earlier + start, 0.0), axis=1, keepdims=True)
        lane1 = lax.broadcasted_iota(jnp.int32, (1, ROUTER_LANES), 1)
        for g in range(N_GROUPS):
            meta_ref[g] = jnp.sum(jnp.where(lane1 == g, start, 0.0)).astype(jnp.int32)
            meta_ref[N_GROUPS + g] = jnp.sum(
                jnp.where(lane1 == g, padded * (1.0 / rb), 0.0)).astype(jnp.int32)
        pos_row = jnp.broadcast_to(pos, (tm, ROUTER_LANES)).T[0:1, :]
        for k in range(tm // chunk):
            rows = slice(k * chunk, (k + 1) * chunk)
            slot = lax.broadcasted_iota(jnp.int32, (chunk, tmp), 1).astype(F32)
            pt_ref[rows, :] = (slot == pos[rows]).astype(BF16)
        for k in range(tmp // chunk):
            rows = slice(k * chunk, (k + 1) * chunk)
            slot = (lax.broadcasted_iota(jnp.int32, (chunk, tm), 0) + k * chunk).astype(F32)
            p_ref[rows, :] = (slot == pos_row).astype(BF16)
        g_hi, g_lo = _split2(gate)
        d = h2_ref.shape[1]
        both = jnp.dot(p_ref[...], jnp.concatenate([h2_ref[...], g_hi, g_lo], axis=1),
                       preferred_element_type=F32)
        hs_ref[...] = both[:, 0:d].astype(BF16)
        gs_ref[...] = both[:, d:d + ROUTER_LANES] + both[:, d + ROUTER_LANES:]
        ys_ref[...] = jnp.zeros_like(ys_ref)

    group = e // EXPERTS_PER_GROUP
    first = meta_ref[group]
    lane = lax.broadcasted_iota(jnp.int32, (rb, ROUTER_LANES), 1)

    def block(j, _):
        rows = pl.ds(pl.multiple_of(first + j * rb, rb), rb)
        h = hs_ref[rows, :]
        gcol = jnp.sum(jnp.where(lane == N_GROUPS + e, gs_ref[rows, :], 0.0), axis=1,
                       keepdims=True)
        a = jnp.dot(h, wg_ref[0], preferred_element_type=F32)
        u = jnp.dot(h, wu_ref[0], preferred_element_type=F32)
        act = (a * _sigmoid(a)) * u * gcol
        ys_ref[rows, :] += jnp.dot(act.astype(BF16), wd_ref[0], preferred_element_type=F32)
        return 0

    lax.fori_loop(0, meta_ref[N_GROUPS + group], block, 0)

    @pl.when(e == pl.num_programs(1) - 1)
    def _():
        pt = pt_ref[...]
        d = y_ref.shape[1]
        for k in range(d // chunk):
            cols = slice(k * chunk, (k + 1) * chunk)
            hi, lo = _split2(ys_ref[:, cols])
            y_ref[:, cols] = (x1_ref[:, cols] + jnp.dot(pt, hi, preferred_element_type=F32)
                              + jnp.dot(pt, lo, preferred_element_type=F32))
        x2 = y_ref[...]
        ms = jnp.mean(x2 * x2, axis=-1, keepdims=True)
        y_ref[...] = x2 * lax.rsqrt(ms + EPS) * fg_ref[...]


def _moe_sorted(x1, h2, gate, wg, wu, wd, final_gain, tm):
    n, d = x1.shape
    ne, _, df = wg.shape
    rb = MOE_ROW_BLOCK
    tmp = tm + N_GROUPS * rb
    row = lambda i, e: (i, 0)
    return pl.pallas_call(
        functools.partial(_moe_sorted_kernel, rb=rb),
        grid=(n // tm, ne),
        in_specs=[
            pl.BlockSpec((tm, d), row),
            pl.BlockSpec((tm, d), row),
            pl.BlockSpec((tm, ROUTER_LANES), row),
            pl.BlockSpec((1, d, df), lambda i, e: (e, 0, 0)),
            pl.BlockSpec((1, d, df), lambda i, e: (e, 0, 0)),
            pl.BlockSpec((1, df, d), lambda i, e: (e, 0, 0)),
            pl.BlockSpec((1, d), lambda i, e: (0, 0)),
        ],
        out_specs=pl.BlockSpec((tm, d), row),
        out_shape=jax.ShapeDtypeStruct((n, d), F32),
        scratch_shapes=[
            pltpu.VMEM((tm, tmp), BF16),
            pltpu.VMEM((tmp, tm), BF16),
            pltpu.VMEM((tmp, d), BF16),
            pltpu.VMEM((tmp, d), F32),
            pltpu.VMEM((tmp, ROUTER_LANES), F32),
            pltpu.SMEM((2 * N_GROUPS,), jnp.int32),
        ],
        compiler_params=_cparams("parallel", "arbitrary"),
        name="moe_sorted",
    )(x1, h2, gate, wg, wu, wd, final_gain.reshape(1, d))


def _tile(n, pref):
    return pref if n % pref == 0 else n


def _router_params(w_rg, b_rg, w_re, b_re):
    d = w_rg.shape[0]
    pad = ROUTER_LANES - N_GROUPS - N_EXPERTS
    w = jnp.concatenate([w_rg, w_re, jnp.zeros((d, pad), F32)], axis=1)
    hi = w.astype(BF16)
    lo = (w - hi.astype(F32)).astype(BF16)
    b = jnp.concatenate([b_rg, b_re, jnp.zeros((pad,), F32)]).reshape(1, ROUTER_LANES)
    return jnp.stack([hi, lo]), b


def _ffn(x2d, oa, ob, sb_gain, w_out_bf16, ffn_gain, w_router_hl, b_router, wg, wu, wd, final_gain):
    n = x2d.shape[0]
    x1, h2, gate = _post(x2d, oa, ob, sb_gain, w_out_bf16, ffn_gain, w_router_hl, b_router,
                         _tile(n, 512))
    if n % MOE_SORT_TILE == 0:
        return _moe_sorted(x1, h2, gate, wg, wu, wd, final_gain, MOE_SORT_TILE)
    return _moe(x1, h2, gate, wg, wu, wd, final_gain, n)


def kernel(x_prompt, x_sample, cache_k, cache_v, state_hgrn, page_table, norm_attn, w_in,
           sb_logit_bias, sb_norm, hg_lb_logits, hg_norm, w_out, norm_ffn, w_router_group,
           b_router_group, w_router_expert, b_router_expert, w_exp_gate, w_exp_up, w_exp_down,
           norm_final):
    depth = w_in.shape[0]
    assert depth == 1, "single-layer trunk"
    l = 0
    bp, t, d = x_prompt.shape
    db, tn, _ = x_sample.shape
    nh, hd = SB_HEADS, SB_HEAD_DIM
    n_pages = page_table.shape[1]

    w_in_b = w_in[l].astype(BF16)
    w_out_b = w_out[l].astype(BF16)
    wr_hl, b_r = _router_params(w_router_group[l], b_router_group[l], w_router_expert[l],
                                b_router_expert[l])
    df = w_exp_gate.shape[-1]
    wg = w_exp_gate[l].reshape(N_EXPERTS, d, df).astype(BF16)
    wu = w_exp_up[l].reshape(N_EXPERTS, d, df).astype(BF16)
    wd = w_exp_down[l].reshape(N_EXPERTS, df, d).astype(BF16)
    bias = sb_logit_bias[l].astype(F32)

    xp = x_prompt.reshape(bp * t, d)
    w3 = 3 * SB_WIDTH
    wq_scaled = w_in[l][:, :SB_WIDTH] * SB_SCALE
    wqkv_t = jnp.concatenate([wq_scaled, w_in[l][:, SB_WIDTH:w3]], axis=1).T.astype(BF16)
    ktp, vtp, qtb, kb, vtb, hgp = _inproj_prompt(
        x_prompt, norm_attn[l], wqkv_t, w_in_b[:, SB_WIDTH:2 * SB_WIDTH], w_in_b[:, w3:],
        SB_QUERY_BLOCK, SB_QUERY_BLOCK)
    oatp = _sb_prompt(qtb, kb, vtb, bias, SB_KEY_BLOCK)
    obp, sp = _hgrn_prompt(hgp, hg_lb_logits, hg_norm[l], l, 512)
    yp = _ffn(xp, oatp, obp.reshape(bp * t, HG_WIDTH), sb_norm[l], w_out_b, norm_ffn[l], wr_hl,
              b_r, wg, wu, wd, norm_final)
    kp = ktp.transpose(0, 3, 1, 2)[None]
    vp = vtp.transpose(0, 3, 1, 2)[None]

    xs = x_sample.reshape(db * tn, d)
    qs, ks, vs, hgs = _inproj(xs, norm_attn[l], w_in_b, 256)
    head_of_lane = jnp.arange(SB_WIDTH) // hd
    onehot = (head_of_lane[None, :] == jnp.arange(nh)[:, None]).astype(F32)
    qbd = (qs.reshape(db, 1, tn, SB_WIDTH) * SB_SCALE) * onehot[None, :, None, :]
    qbd = qbd.reshape(db, nh * tn, SB_WIDTH).astype(BF16)
    bias_rows = jnp.broadcast_to(jnp.repeat(bias, tn)[:, None], (nh * tn, PAGE_SIZE))
    pad = ((0, 0), (0, 8 - tn), (0, 0))
    kn = jnp.pad(ks.reshape(db, tn, SB_WIDTH), pad)
    vn = jnp.pad(vs.reshape(db, tn, SB_WIDTH), pad)
    ck = cache_k[l].transpose(0, 2, 3, 1).reshape(-1, SB_WIDTH, PAGE_SIZE)
    cv = cache_v[l].transpose(0, 2, 3, 1).reshape(-1, SB_WIDTH, PAGE_SIZE)
    o_full = _sb_sample(qbd, bias_rows, kn, vn, ck, cv, page_table.reshape(-1), n_pages, tn, 16)
    oas = jnp.sum(o_full.reshape(db, nh, tn, SB_WIDTH) * onehot[None, :, None, :], axis=1)
    obs, ss = _hgrn_sample(hgs.reshape(db, tn, -1), state_hgrn[l], hg_lb_logits, hg_norm[l], l)
    ys = _ffn(xs, oas.reshape(db * tn, SB_WIDTH), obs.reshape(db * tn, HG_WIDTH), sb_norm[l],
              w_out_b, norm_ffn[l], wr_hl, b_r, wg, wu, wd, norm_final)

    sdt = state_hgrn.dtype
    return (yp.reshape(bp, t, d), ys.reshape(db, tn, d), kp, vp, sp.astype(sdt)[None],
            ks.reshape(1, db, tn, nh, hd), vs.reshape(1, db, tn, nh, hd), ss.astype(sdt)[None])
```

```python
import functools

import jax
import jax.numpy as jnp
from jax import lax
from jax.experimental import pallas as pl
from jax.experimental.pallas import tpu as pltpu

F32 = jnp.float32
BF16 = jnp.bfloat16
EPS = 1e-6
NEG_LOG2E = -1.4426950408889634

SB_HEADS = 8
SB_HEAD_DIM = 64
SB_WIDTH = SB_HEADS * SB_HEAD_DIM
SB_SCALE = SB_HEAD_DIM ** -0.5
HG_HEADS = 4
HG_DK = 128
HG_DV = 128
HG_WIDTH = HG_HEADS * HG_DK
HG_CHUNK = 64
HG_SUB = 16
N_GROUPS = 4
EXPERTS_PER_GROUP = 8
N_EXPERTS = N_GROUPS * EXPERTS_PER_GROUP
ROUTER_LANES = 128
GROUP_ID_LANE = ROUTER_LANES - 1
MOE_ROW_BLOCK = 256
MOE_EXPERTS_PER_STEP = 2
MOE_SORT_TILE = 1024
PAGE_SIZE = 128
SB_KEY_BLOCK = 128
SB_QUERY_BLOCK = 512

VMEM_LIMIT_BYTES = 56 * 1024 * 1024

NT_DIMS = (((1,), (1,)), ((), ()))
TN_DIMS = (((0,), (0,)), ((), ()))


def _cparams(*sem):
    return pltpu.CompilerParams(dimension_semantics=sem, vmem_limit_bytes=VMEM_LIMIT_BYTES)


def _split2(x):
    hi = x.astype(BF16)
    lo = (x - hi.astype(F32)).astype(BF16)
    return hi, lo


def _split3(x):
    hi = x.astype(BF16)
    r = x - hi.astype(F32)
    mid = r.astype(BF16)
    lo = (r - mid.astype(F32)).astype(BF16)
    return hi, mid, lo


def _softplus(z):
    return jnp.maximum(z, 0.0) + jnp.log(1.0 + jnp.exp2(jnp.abs(z) * NEG_LOG2E))


def _sigmoid(x):
    return 1.0 / (1.0 + jnp.exp(-x))


def _inproj_kernel(x_ref, g_ref, w_ref, q_ref, k_ref, v_ref, hg_ref):
    x = x_ref[...]
    ms = jnp.mean(x * x, axis=-1, keepdims=True)
    h = (x * lax.rsqrt(ms + EPS) * g_ref[...]).astype(BF16)
    w = SB_WIDTH
    q_ref[...] = jnp.dot(h, w_ref[:, 0:w], preferred_element_type=F32)
    k_ref[...] = jnp.dot(h, w_ref[:, w:2 * w], preferred_element_type=F32)
    v_ref[...] = jnp.dot(h, w_ref[:, 2 * w:3 * w], preferred_element_type=F32)
    hg_ref[...] = jnp.dot(h, w_ref[:, 3 * w:], preferred_element_type=F32)


def _inproj(x2d, gain, w_in_bf16, tm):
    n, d = x2d.shape
    e = w_in_bf16.shape[1]
    hgw = e - 3 * SB_WIDTH
    return pl.pallas_call(
        _inproj_kernel,
        grid=(n // tm,),
        in_specs=[
            pl.BlockSpec((tm, d), lambda i: (i, 0)),
            pl.BlockSpec((1, d), lambda i: (0, 0)),
            pl.BlockSpec((d, e), lambda i: (0, 0)),
        ],
        out_specs=[
            pl.BlockSpec((tm, SB_WIDTH), lambda i: (i, 0)),
            pl.BlockSpec((tm, SB_WIDTH), lambda i: (i, 0)),
            pl.BlockSpec((tm, SB_WIDTH), lambda i: (i, 0)),
            pl.BlockSpec((tm, hgw), lambda i: (i, 0)),
        ],
        out_shape=[
            jax.ShapeDtypeStruct((n, SB_WIDTH), F32),
            jax.ShapeDtypeStruct((n, SB_WIDTH), F32),
            jax.ShapeDtypeStruct((n, SB_WIDTH), F32),
            jax.ShapeDtypeStruct((n, hgw), F32),
        ],
        compiler_params=_cparams("parallel"),
        name="inproj",
    )(x2d, gain.reshape(1, d), w_in_bf16)


def _inproj_prompt_kernel(x_ref, g_ref, wt_ref, wk_ref, whg_ref,
                          kt_ref, vt_ref, qtb_ref, kb_ref, vtb_ref, hg_ref, *, tk):
    x = x_ref[0]
    tm = x.shape[0]
    ms = jnp.mean(x * x, axis=-1, keepdims=True)
    h = (x * lax.rsqrt(ms + EPS) * g_ref[...]).astype(BF16)
    qkvt = lax.dot_general(wt_ref[...], h, NT_DIMS, preferred_element_type=F32)
    k = jnp.dot(h, wk_ref[...], preferred_element_type=F32)
    w, hd = SB_WIDTH, SB_HEAD_DIM
    for hh in range(SB_HEADS):
        r = hh * hd
        qtb_ref[0, hh] = qkvt[r:r + hd].astype(BF16)
        kt_ref[0, hh] = qkvt[w + r:w + r + hd]
        vt = qkvt[2 * w + r:2 * w + r + hd]
        vt_ref[0, hh] = vt
        for c in range(tm // tk):
            vtb_ref[0, hh, c] = vt[:, c * tk:(c + 1) * tk].astype(BF16)
        kb_ref[0, hh] = k[:, r:r + hd].astype(BF16)
    hg_ref[0] = jnp.dot(h, whg_ref[...], preferred_element_type=F32)


def _inproj_prompt(x, gain, wqkv_t, wk, whg, tm, tk):
    b, t, d = x.shape
    nh, hd, w = SB_HEADS, SB_HEAD_DIM, SB_WIDTH
    hgw = whg.shape[1]
    fixed = lambda bi, i: (0, 0)
    tspec = pl.BlockSpec((1, nh, hd, tm), lambda bi, i: (bi, 0, 0, i))
    return pl.pallas_call(
        functools.partial(_inproj_prompt_kernel, tk=tk),
        grid=(b, t // tm),
        in_specs=[
            pl.BlockSpec((1, tm, d), lambda bi, i: (bi, i, 0)),
            pl.BlockSpec((1, d), fixed),
            pl.BlockSpec((3 * w, d), fixed),
            pl.BlockSpec((d, w), fixed),
            pl.BlockSpec((d, hgw), fixed),
        ],
        out_specs=[
            tspec, tspec, tspec,
            pl.BlockSpec((1, nh, tm, hd), lambda bi, i: (bi, 0, i, 0)),
            pl.BlockSpec((1, nh, tm // tk, hd, tk), lambda bi, i: (bi, 0, i, 0, 0)),
            pl.BlockSpec((1, tm, hgw), lambda bi, i: (bi, i, 0)),
        ],
        out_shape=[
            jax.ShapeDtypeStruct((b, nh, hd, t), F32),
            jax.ShapeDtypeStruct((b, nh, hd, t), F32),
            jax.ShapeDtypeStruct((b, nh, hd, t), BF16),
            jax.ShapeDtypeStruct((b, nh, t, hd), BF16),
            jax.ShapeDtypeStruct((b, nh, t // tk, hd, tk), BF16),
            jax.ShapeDtypeStruct((b, t, hgw), F32),
        ],
        compiler_params=_cparams("parallel", "parallel"),
        name="inproj_prompt",
    )(x, gain.reshape(1, d), wqkv_t, wk, whg)


def _sb_row_weights(z, tri2, carry, n, before=None):
    sp = _softplus(z)
    if before is not None:
        sp = jnp.where(before, sp, 0.0)
        z = jnp.where(before, z, -jnp.inf)
    hi, lo = _split2(sp)
    blocks = []
    for p in range(z.shape[1] // n):
        sl = slice(p * n, (p + 1) * n)
        cs = jnp.dot(jnp.concatenate([hi[:, sl], lo[:, sl]], axis=1), tri2,
                     preferred_element_type=F32)
        blocks.append(jnp.exp(z[:, sl] - cs - carry).astype(BF16))
        carry = carry + cs[:, 0:1]
    return jnp.concatenate(blocks, axis=1), carry


def _tri_incl(n):
    j = lax.broadcasted_iota(jnp.int32, (n, n), 0)
    s = lax.broadcasted_iota(jnp.int32, (n, n), 1)
    return (j >= s).astype(BF16)


def _sb_prompt_kernel(bias_ref, qt_ref, k_ref, vt_ref, ot_ref, z0_ref, hl0_ref, z1_ref, hl1_ref,
                      *, tq, tk):
    h = pl.program_id(1)
    i = pl.program_id(2)
    bias = bias_ref[h]
    qt = qt_ref[0, 0]
    nb = tq // tk
    r = lax.broadcasted_iota(jnp.int32, (tk, tk), 0)
    c = lax.broadcasted_iota(jnp.int32, (tk, tk), 1)
    triu = (c >= r).astype(BF16)
    tri2 = jnp.concatenate([triu, triu], axis=1)
    bufs = ((z0_ref, hl0_ref), (z1_ref, hl1_ref))

    def scores(g, slot, masked):
        z_ref, hl_ref = bufs[slot]
        rows = k_ref[0, 0, pl.ds(pl.multiple_of(g * tq, tq), tq), :]
        z = jnp.dot(rows, qt, preferred_element_type=F32) + bias
        sp = _softplus(z)
        if masked:
            s_loc = lax.broadcasted_iota(jnp.int32, (tq, tq), 0)
            t_loc = lax.broadcasted_iota(jnp.int32, (tq, tq), 1)
            before = s_loc < t_loc
            sp = jnp.where(before, sp, 0.0)
            z = jnp.where(before, z, -jnp.inf)
        z_ref[...] = z
        hi, lo = _split2(sp)
        for p in range(nb):
            hl_ref[p, 0:tk, :] = hi[p * tk:(p + 1) * tk]
            hl_ref[p, tk:2 * tk, :] = lo[p * tk:(p + 1) * tk]

    def weights(g, slot, carry, acc):
        z_ref, hl_ref = bufs[slot]
        a_blocks = [None] * nb
        for p in reversed(range(nb)):
            cs = jnp.dot(tri2, hl_ref[p], preferred_element_type=F32)
            la = z_ref[p * tk:(p + 1) * tk, :] - cs - carry
            a_blocks[p] = jnp.exp(la).astype(BF16)
            carry = carry + cs[0:1, :]
        a = jnp.concatenate(a_blocks, axis=0)
        return carry, acc + jnp.dot(vt_ref[0, 0, g], a, preferred_element_type=F32)

    scores(i, 0, True)

    def pair(m, c):
        g = i - 2 * m
        scores(g - 1, 1, False)
        c = weights(g, 0, *c)
        scores(g - 2, 0, False)
        return weights(g - 1, 1, *c)

    init = (jnp.zeros((1, tq), F32), jnp.zeros((SB_HEAD_DIM, tq), F32))
    c = lax.fori_loop(0, i // 2, pair, init)

    def odd_tail(c):
        scores(0, 1, False)
        return weights(0, 1, *weights(1, 0, *c))

    carry, acc = lax.cond(i % 2 == 1, odd_tail, lambda c: weights(0, 0, *c), c)
    ot_ref[0, 0] = acc


def _sb_prompt(qt, k, vt, bias, tk):
    b, h, d, t = qt.shape
    tq = vt.shape[-1]
    return pl.pallas_call(
        functools.partial(_sb_prompt_kernel, tq=tq, tk=tk),
        grid=(b, h, t // tq),
        in_specs=[
            pl.BlockSpec(memory_space=pltpu.SMEM),
            pl.BlockSpec((1, 1, d, tq), lambda bi, hi, i: (bi, hi, 0, i)),
            pl.BlockSpec((1, 1, t, d), lambda bi, hi, i: (bi, hi, 0, 0)),
            pl.BlockSpec((1, 1, t // tq, d, tq), lambda bi, hi, i: (bi, hi, 0, 0, 0)),
        ],
        out_specs=pl.BlockSpec((1, 1, d, tq), lambda bi, hi, i: (bi, hi, 0, i)),
        out_shape=jax.ShapeDtypeStruct((b, h, d, t), F32),
        scratch_shapes=[
            pltpu.VMEM((tq, tq), F32),
            pltpu.VMEM((tq // tk, 2 * tk, tq), BF16),
        ] * 2,
        compiler_params=_cparams("parallel", "parallel", "arbitrary"),
        name="sb_prompt",
    )(bias, qt, k, vt)


def _sb_sample_kernel(pt_ref, qbd_ref, bias_ref, kn_ref, vn_ref, *rest, pages_per_step, t_new):
    g = pages_per_step
    k_refs = rest[:g]
    v_refs = rest[g:2 * g]
    o_ref = rest[2 * g]
    carry_ref, acc_ref, kpad_ref, vpad_ref = rest[2 * g + 1:]
    step_id = pl.program_id(1)
    rows = qbd_ref.shape[1]
    n = PAGE_SIZE
    qbd = qbd_ref[0]
    bias = bias_ref[...]
    tri = _tri_incl(n)
    tri2 = jnp.concatenate([tri, tri], axis=0)

    @pl.when(step_id == 0)
    def _():
        kpad_ref[...] = jnp.zeros_like(kpad_ref)
        vpad_ref[...] = jnp.zeros_like(vpad_ref)
        kpad_ref[0:kn_ref.shape[1], :] = kn_ref[0]
        vpad_ref[0:vn_ref.shape[1], :] = vn_ref[0]
        qi = lax.broadcasted_iota(jnp.int32, (rows, n), 0) % t_new
        s = lax.broadcasted_iota(jnp.int32, (rows, n), 1)
        z = lax.dot_general(qbd, kpad_ref[...].astype(BF16), NT_DIMS,
                            preferred_element_type=F32) + bias
        a, carry = _sb_row_weights(z, tri2, jnp.zeros((rows, 1), F32), n, s < qi)
        carry_ref[...] = jnp.broadcast_to(carry, carry_ref.shape)
        acc_ref[...] = jnp.dot(a, vpad_ref[...].astype(BF16), preferred_element_type=F32)

    kcat = jnp.concatenate([k_refs[p][0].astype(BF16) for p in range(g)], axis=1)
    vcat = jnp.concatenate([v_refs[p][0].astype(BF16) for p in range(g)], axis=1)
    z = jnp.dot(qbd, kcat, preferred_element_type=F32) + jnp.tile(bias, (1, g))
    a, carry = _sb_row_weights(z, tri2, carry_ref[:, 0:1], n)
    acc = acc_ref[...] + lax.dot_general(a, vcat, NT_DIMS, preferred_element_type=F32)
    carry_ref[...] = jnp.broadcast_to(carry, carry_ref.shape)
    acc_ref[...] = acc

    @pl.when(step_id == pl.num_programs(1) - 1)
    def _():
        o_ref[0] = acc


def _sb_sample(qbd, bias_rows, k_new, v_new, cache_k, cache_v, page_table_flat, n_pages, t_new,
               pages_per_step):
    b, rows, w = qbd.shape
    g = pages_per_step
    steps = n_pages // g

    def page_map(p):
        return lambda bi, si, pt: (pt[bi * n_pages + (n_pages - 1 - (si * g + p))], 0, 0)

    page_specs = [pl.BlockSpec((1, w, PAGE_SIZE), page_map(p)) for p in range(g)]
    grid_spec = pltpu.PrefetchScalarGridSpec(
        num_scalar_prefetch=1,
        grid=(b, steps),
        in_specs=[
            pl.BlockSpec((1, rows, w), lambda bi, si, pt: (bi, 0, 0)),
            pl.BlockSpec((rows, PAGE_SIZE), lambda bi, si, pt: (0, 0)),
            pl.BlockSpec((1, k_new.shape[1], w), lambda bi, si, pt: (bi, 0, 0)),
            pl.BlockSpec((1, v_new.shape[1], w), lambda bi, si, pt: (bi, 0, 0)),
        ] + page_specs + page_specs,
        out_specs=pl.BlockSpec((1, rows, w), lambda bi, si, pt: (bi, 0, 0)),
        scratch_shapes=[
            pltpu.VMEM((rows, PAGE_SIZE), F32),
            pltpu.VMEM((rows, w), F32),
            pltpu.VMEM((PAGE_SIZE, w), F32),
            pltpu.VMEM((PAGE_SIZE, w), F32),
        ],
    )
    return pl.pallas_call(
        functools.partial(_sb_sample_kernel, pages_per_step=g, t_new=t_new),
        grid_spec=grid_spec,
        out_shape=jax.ShapeDtypeStruct((b, rows, w), F32),
        compiler_params=_cparams("parallel", "arbitrary"),
        name="sb_sample",
    )(page_table_flat, qbd, bias_rows, k_new, v_new, *([cache_k] * g), *([cache_v] * g))


def _lower_bound(lb_logits, layer):
    m = jnp.max(lb_logits, axis=0, keepdims=True)
    e = jnp.exp(lb_logits - m)
    return jnp.sum(e[0:layer + 1], axis=0, keepdims=True) / jnp.sum(e, axis=0, keepdims=True)


def _hg_out(o, gain, gb):
    ms = jnp.mean(o * o, axis=-1, keepdims=True)
    return (o * lax.rsqrt(ms + EPS) * gain) * (gb * _sigmoid(gb))


def _hgrn_prompt_kernel(lbl_ref, gain_ref, hg_ref, o_ref, s_ref, st_ref, *, layer, n_chunks):
    c = HG_CHUNK
    w = HG_WIDTH
    half = HG_SUB // 2
    heads = range(HG_HEADS)
    t_id = pl.program_id(1)

    @pl.when(t_id == 0)
    def _():
        st_ref[...] = jnp.zeros_like(st_ref)

    lb_all = _lower_bound(lbl_ref[...], layer)
    gain = gain_ref[...]
    ti = lax.broadcasted_iota(jnp.int32, (c, c), 0)
    si = lax.broadcasted_iota(jnp.int32, (c, c), 1)
    ltri = (ti >= si).astype(BF16)
    ltri3 = jnp.concatenate([ltri, ltri, ltri], axis=1)
    rowc = lax.broadcasted_iota(jnp.int32, (c, HG_DK), 0)
    rowh = lax.broadcasted_iota(jnp.int32, (half, HG_DK), 0)

    def direct(g_q, q_q, g_k, k_k, v_k, causal):
        od = jnp.zeros((half, HG_DV), F32)
        for s in range(half):
            d = g_q - g_k[s:s + 1, :]
            if causal:
                d = jnp.where(rowh >= s, d, -jnp.inf)
            a = jnp.sum(q_q * (k_k[s:s + 1, :] * jnp.exp(d)), axis=1, keepdims=True)
            od = od + a * v_k[s:s + 1, :]
        return od

    def chunk(n, _):
        sl = pl.ds(pl.multiple_of(n * c, c), c)

        def col(which, h):
            lo = which * w + h * HG_DK
            return hg_ref[0, sl, lo:lo + HG_DK]

        f = [lb_all[:, h * HG_DK:(h + 1) * HG_DK]
             + (1.0 - lb_all[:, h * HG_DK:(h + 1) * HG_DK]) * _sigmoid(col(1, h)) for h in heads]
        kk = [1.0 - f[h] for h in heads]
        qq = [col(0, h) * _sigmoid(col(0, h)) for h in heads]
        v = [col(2, h) for h in heads]
        vb = [v[h].astype(BF16) for h in heads]
        gcum = [jnp.dot(ltri3, jnp.concatenate(_split3(jnp.log(f[h])), axis=0),
                        preferred_element_type=F32) for h in heads]
        g_last = [gcum[h][c - 1:c, :] for h in heads]
        st = [st_ref[h] for h in heads]
        o = [lax.dot_general((qq[h] * jnp.exp(gcum[h])).astype(BF16), st[h].astype(BF16),
                             NT_DIMS, preferred_element_type=F32) for h in heads]
        parts = [[jnp.zeros((HG_SUB, HG_DV), F32)] for _ in heads]
        for i in range(1, c // HG_SUB):
            lo = i * HG_SUB
            att = []
            for h in heads:
                r = gcum[h][lo - 1:lo, :]
                qh = qq[h][lo:lo + HG_SUB] * jnp.exp(gcum[h][lo:lo + HG_SUB] - r)
                kh = kk[h] * jnp.exp(jnp.where(rowc < lo, r - gcum[h], -jnp.inf))
                att.append(lax.dot_general(qh.astype(BF16), kh.astype(BF16), NT_DIMS,
                                           preferred_element_type=F32))
            for h in heads:
                parts[h].append(jnp.dot(att[h].astype(BF16), vb[h], preferred_element_type=F32))
        for h in heads:
            rows_out = []
            for i in range(c // HG_SUB):
                top = slice(i * HG_SUB, i * HG_SUB + half)
                bot = slice(i * HG_SUB + half, (i + 1) * HG_SUB)
                g, q, k, vv = gcum[h], qq[h], kk[h], v[h]
                od_top = direct(g[top], q[top], g[top], k[top], vv[top], True)
                od_bot = (direct(g[bot], q[bot], g[top], k[top], vv[top], False)
                          + direct(g[bot], q[bot], g[bot], k[bot], vv[bot], True))
                rows_out.append(jnp.concatenate([od_top, od_bot], axis=0) + parts[h][i])
            oh = o[h] + jnp.concatenate(rows_out, axis=0)
            o_ref[0, sl, h * HG_DV:(h + 1) * HG_DV] = _hg_out(oh, gain, col(3, h))
        for h in heads:
            ke = kk[h] * jnp.exp(g_last[h] - gcum[h])
            st_ref[h] = st[h] * jnp.exp(g_last[h]) + lax.dot_general(
                vb[h], ke.astype(BF16), TN_DIMS, preferred_element_type=F32)
        return 0

    lax.fori_loop(0, n_chunks, chunk, 0)

    @pl.when(t_id == pl.num_programs(1) - 1)
    def _():
        for h in heads:
            s_ref[0, h] = st_ref[h].T


def _hgrn_prompt(hg, lb_logits, gain, layer, tt):
    b, t, e = hg.shape
    nh = HG_HEADS
    return pl.pallas_call(
        functools.partial(_hgrn_prompt_kernel, layer=layer, n_chunks=tt // HG_CHUNK),
        grid=(b, t // tt),
        in_specs=[
            pl.BlockSpec(lb_logits.shape, lambda bi, ti: (0, 0)),
            pl.BlockSpec((1, HG_DV), lambda bi, ti: (0, 0)),
            pl.BlockSpec((1, tt, e), lambda bi, ti: (bi, ti, 0)),
        ],
        out_specs=[
            pl.BlockSpec((1, tt, HG_WIDTH), lambda bi, ti: (bi, ti, 0)),
            pl.BlockSpec((1, nh, HG_DK, HG_DV), lambda bi, ti: (bi, 0, 0, 0)),
        ],
        out_shape=[
            jax.ShapeDtypeStruct((b, t, HG_WIDTH), F32),
            jax.ShapeDtypeStruct((b, nh, HG_DK, HG_DV), F32),
        ],
        scratch_shapes=[pltpu.VMEM((nh, HG_DV, HG_DK), F32)],
        compiler_params=_cparams("parallel", "arbitrary"),
        name="hgrn_prompt",
    )(lb_logits, gain.reshape(1, HG_DV), hg)


def _hgrn_sample_kernel(lbl_ref, gain_ref, hg_ref, s0_ref, o_ref, s_ref, *, layer, t_new):
    w = HG_WIDTH
    mxu_rows = 16
    lb_all = _lower_bound(lbl_ref[...], layer)
    gain = gain_ref[...]
    x = hg_ref[0]
    rowt = lax.broadcasted_iota(jnp.int32, (t_new, HG_DK), 0)

    def padded(a):
        return jnp.concatenate([a, jnp.zeros((mxu_rows - t_new, a.shape[1]), F32)], axis=0)

    outs = []
    for h in range(HG_HEADS):
        c0 = h * HG_DK
        lb = lb_all[:, c0:c0 + HG_DK]
        qx = x[:, c0:c0 + HG_DK]
        qq = qx * _sigmoid(qx)
        f = lb + (1.0 - lb) * _sigmoid(x[:, w + c0:w + c0 + HG_DK])
        kk = 1.0 - f
        v = x[:, 2 * w + c0:2 * w + c0 + HG_DV]
        gb = x[:, 3 * w + c0:3 * w + c0 + HG_DV]
        logf = jnp.log(f)
        g_rows = [logf[0:1, :]]
        for t in range(1, t_new):
            g_rows.append(g_rows[-1] + logf[t:t + 1, :])
        gcum = jnp.concatenate(g_rows, axis=0)
        g_last = g_rows[-1]
        s0 = s0_ref[0, h]
        o = jnp.dot(padded(qq * jnp.exp(gcum)).astype(BF16), s0.astype(BF16),
                    preferred_element_type=F32)[0:t_new]
        for s in range(t_new):
            d = jnp.where(rowt >= s, gcum - g_rows[s], -jnp.inf)
            a = jnp.sum(qq * (kk[s:s + 1, :] * jnp.exp(d)), axis=1, keepdims=True)
            o = o + a * v[s:s + 1, :]
        ke = padded(kk * jnp.exp(g_last - gcum)).astype(BF16)
        outer = lax.dot_general(ke, padded(v).astype(BF16), TN_DIMS, preferred_element_type=F32)
        e_rows = jnp.concatenate([jnp.exp(g_last), jnp.zeros((HG_DK - 1, HG_DK), F32)], axis=0)
        s_ref[0, h] = e_rows.T[:, 0:1] * s0 + outer
        outs.append(_hg_out(o, gain, gb))
    o_ref[0] = jnp.concatenate(outs, axis=1)


def _hgrn_sample(hg, state, lb_logits, gain, layer):
    b, t_new, e = hg.shape
    nh = HG_HEADS
    return pl.pallas_call(
        functools.partial(_hgrn_sample_kernel, layer=layer, t_new=t_new),
        grid=(b,),
        in_specs=[
            pl.BlockSpec(lb_logits.shape, lambda bi: (0, 0)),
            pl.BlockSpec((1, HG_DV), lambda bi: (0, 0)),
            pl.BlockSpec((1, t_new, e), lambda bi: (bi, 0, 0)),
            pl.BlockSpec((1, nh, HG_DK, HG_DV), lambda bi: (bi, 0, 0, 0)),
        ],
        out_specs=[
            pl.BlockSpec((1, t_new, HG_WIDTH), lambda bi: (bi, 0, 0)),
            pl.BlockSpec((1, nh, HG_DK, HG_DV), lambda bi: (bi, 0, 0, 0)),
        ],
        out_shape=[
            jax.ShapeDtypeStruct((b, t_new, HG_WIDTH), F32),
            jax.ShapeDtypeStruct((b, nh, HG_DK, HG_DV), F32),
        ],
        compiler_params=_cparams("parallel"),
        name="hgrn_sample",
    )(lb_logits, gain.reshape(1, HG_DV), hg, state)


def _route(logits):
    lane = lax.broadcasted_iota(jnp.int32, logits.shape, 1)
    neg = -jnp.inf
    big = ROUTER_LANES

    def top(mask):
        val = jnp.max(jnp.where(mask, logits, neg), axis=1, keepdims=True)
        idx = jnp.min(jnp.where(mask & (logits == val), lane, big), axis=1, keepdims=True)
        return val, idx

    is_group = lane < N_GROUPS
    gmax, gidx = top(is_group)
    p_top = 1.0 / jnp.sum(jnp.where(is_group, jnp.exp(logits - gmax), 0.0), axis=1, keepdims=True)
    lo = N_GROUPS + gidx * EXPERTS_PER_GROUP
    in_group = (lane >= lo) & (lane < lo + EXPERTS_PER_GROUP)
    v1, i1 = top(in_group)
    v2, i2 = top(in_group & (lane != i1))
    e2 = jnp.exp(v2 - v1)
    w1 = p_top / (1.0 + e2)
    w2 = p_top * e2 / (1.0 + e2)
    gate = jnp.where(lane == i1, w1, 0.0) + jnp.where(lane == i2, w2, 0.0)
    return jnp.where(lane == GROUP_ID_LANE, gidx.astype(F32), gate)


def _post_kernel(x_ref, oa_ref, ob_ref, sbg_ref, wo_ref, ng_ref, wr_ref, br_ref,
                 x1_ref, h2_ref, gate_ref, *, oa_transposed):
    w = SB_WIDTH
    if oa_transposed:
        oa = jnp.concatenate([oa_ref[0, hh] for hh in range(SB_HEADS)], axis=0)
        ms = jnp.mean(oa * oa, axis=0, keepdims=True)
        oa = (oa * lax.rsqrt(ms + EPS) * sbg_ref[...]).astype(BF16)
        mixed = lax.dot_general(oa, wo_ref[0:w, :], TN_DIMS, preferred_element_type=F32)
    else:
        oa = oa_ref[...]
        ms = jnp.mean(oa * oa, axis=-1, keepdims=True)
        oa = (oa * lax.rsqrt(ms + EPS) * sbg_ref[...]).astype(BF16)
        mixed = jnp.dot(oa, wo_ref[0:w, :], preferred_element_type=F32)
    mixed = mixed + jnp.dot(ob_ref[...].astype(BF16), wo_ref[w:, :], preferred_element_type=F32)
    x1 = x_ref[...] + mixed
    x1_ref[...] = x1
    ms = jnp.mean(x1 * x1, axis=-1, keepdims=True)
    h2 = x1 * lax.rsqrt(ms + EPS) * ng_ref[...]
    h2_ref[...] = h2.astype(BF16)
    hh, hl = _split2(h2)
    logits = (jnp.dot(hh, wr_ref[0], preferred_element_type=F32)
              + jnp.dot(hl, wr_ref[0], preferred_element_type=F32)
              + jnp.dot(hh, wr_ref[1], preferred_element_type=F32)) + br_ref[...]
    gate_ref[...] = _route(logits)


def _post(x2d, oa, ob, sb_gain, w_out_bf16, ffn_gain, w_router_hl, b_router, tm):
    n, d = x2d.shape
    row = lambda i: (i, 0)
    fixed = lambda i: (0, 0)
    oa_transposed = oa.ndim == 4
    if oa_transposed:
        per_b = oa.shape[3] // tm
        oa_spec = pl.BlockSpec((1, SB_HEADS, SB_HEAD_DIM, tm),
                               lambda i: (i // per_b, 0, 0, i % per_b))
        sb_gain = sb_gain.reshape(SB_WIDTH, 1)
    else:
        oa_spec = pl.BlockSpec((tm, SB_WIDTH), row)
        sb_gain = sb_gain.reshape(1, SB_WIDTH)
    return pl.pallas_call(
        functools.partial(_post_kernel, oa_transposed=oa_transposed),
        grid=(n // tm,),
        in_specs=[
            pl.BlockSpec((tm, d), row),
            oa_spec,
            pl.BlockSpec((tm, HG_WIDTH), row),
            pl.BlockSpec(sb_gain.shape, fixed),
            pl.BlockSpec(w_out_bf16.shape, fixed),
            pl.BlockSpec((1, d), fixed),
            pl.BlockSpec(w_router_hl.shape, lambda i: (0, 0, 0)),
            pl.BlockSpec((1, ROUTER_LANES), fixed),
        ],
        out_specs=[
            pl.BlockSpec((tm, d), row),
            pl.BlockSpec((tm, d), row),
            pl.BlockSpec((tm, ROUTER_LANES), row),
        ],
        out_shape=[
            jax.ShapeDtypeStruct((n, d), F32),
            jax.ShapeDtypeStruct((n, d), BF16),
            jax.ShapeDtypeStruct((n, ROUTER_LANES), F32),
        ],
        compiler_params=_cparams("parallel"),
        name="post",
    )(x2d, oa, ob, sb_gain, w_out_bf16, ffn_gain.reshape(1, d), w_router_hl, b_router)


def _gated_act(h, wg, wu, gates, step):
    a = jnp.dot(h, wg, preferred_element_type=F32)
    u = jnp.dot(h, wu, preferred_element_type=F32)
    lane = lax.broadcasted_iota(jnp.int32, gates.shape, 1)
    df = a.shape[1] // MOE_EXPERTS_PER_STEP
    parts = []
    for k in range(MOE_EXPERTS_PER_STEP):
        e = step * MOE_EXPERTS_PER_STEP + k
        gcol = jnp.sum(jnp.where(lane == N_GROUPS + e, gates, 0.0), axis=1, keepdims=True)
        ak = a[:, k * df:(k + 1) * df]
        parts.append(((ak * _sigmoid(ak)) * u[:, k * df:(k + 1) * df] * gcol).astype(BF16))
    return jnp.concatenate(parts, axis=1)


def _moe_kernel(x1_ref, h2_ref, gate_ref, wg_ref, wu_ref, wd_ref, fg_ref, y_ref, acc_ref):
    e = pl.program_id(1)

    @pl.when(e == 0)
    def _():
        acc_ref[...] = jnp.zeros_like(acc_ref)

    act = _gated_act(h2_ref[...], wg_ref[0], wu_ref[0], gate_ref[...], e)
    acc_ref[...] += jnp.dot(act, wd_ref[0], preferred_element_type=F32)

    @pl.when(e == pl.num_programs(1) - 1)
    def _():
        x2 = x1_ref[...] + acc_ref[...]
        ms = jnp.mean(x2 * x2, axis=-1, keepdims=True)
        y_ref[...] = x2 * lax.rsqrt(ms + EPS) * fg_ref[...]


def _moe(x1, h2, gate, wg, wu, wd, final_gain, tm):
    n, d = x1.shape
    ne, _, df = wg.shape
    row = lambda i, e: (i, 0)
    return pl.pallas_call(
        _moe_kernel,
        grid=(n // tm, ne),
        in_specs=[
            pl.BlockSpec((tm, d), row),
            pl.BlockSpec((tm, d), row),
            pl.BlockSpec((tm, ROUTER_LANES), row),
            pl.BlockSpec((1, d, df), lambda i, e: (e, 0, 0)),
            pl.BlockSpec((1, d, df), lambda i, e: (e, 0, 0)),
            pl.BlockSpec((1, df, d), lambda i, e: (e, 0, 0)),
            pl.BlockSpec((1, d), lambda i, e: (0, 0)),
        ],
        out_specs=pl.BlockSpec((tm, d), row),
        out_shape=jax.ShapeDtypeStruct((n, d), F32),
        scratch_shapes=[pltpu.VMEM((tm, d), F32)],
        compiler_params=_cparams("parallel", "arbitrary"),
        name="moe",
    )(x1, h2, gate, wg, wu, wd, final_gain.reshape(1, d))


def _moe_sorted_kernel(x1_ref, h2_ref, gate_ref, wg_ref, wu_ref, wd_ref, fg_ref, y_ref,
                       pt_ref, hs_ref, ys_ref, gs_ref, meta_ref, *, rb):
    e = pl.program_id(1)
    tm = h2_ref.shape[0]
    tmp = hs_ref.shape[0]
    chunk = 256

    @pl.when(e == 0)
    def _():
        gate = gate_ref[...]
        lane = lax.broadcasted_iota(jnp.int32, gate.shape, 1)
        onehot = lane.astype(F32) == gate[:, GROUP_ID_LANE:GROUP_ID_LANE + 1]
        r = lax.broadcasted_iota(jnp.int32, (tm, tm), 0)
        c = lax.broadcasted_iota(jnp.int32, (tm, tm), 1)
        earlier = jnp.dot((c < r).astype(BF16), onehot.astype(BF16),
                          preferred_element_type=F32)
        count = earlier[tm - 1:tm, :] + onehot[tm - 1:tm, :].astype(F32)
        padded = jnp.ceil(count * (1.0 / rb)) * rb
        g0 = lax.broadcasted_iota(jnp.int32, (ROUTER_LANES, ROUTER_LANES), 0)
        g1 = lax.broadcasted_iota(jnp.int32, (ROUTER_LANES, ROUTER_LANES), 1)
        start = jnp.dot(jnp.broadcast_to(padded, (8, ROUTER_LANES)).astype(BF16),
                        (g0 < g1).astype(BF16), preferred_element_type=F32)[0:1, :]
        pos = jnp.sum(jnp.where(onehot, earlier + start, 0.0), axis=1, keepdims=True)
        lane1 = lax.broadcasted_iota(jnp.int32, (1, ROUTER_LANES), 1)
        for g in range(N_GROUPS):
            meta_ref[g] = jnp.sum(jnp.where(lane1 == g, start, 0.0)).astype(jnp.int32)
            meta_ref[N_GROUPS + g] = jnp.sum(
                jnp.where(lane1 == g, padded * (1.0 / rb), 0.0)).astype(jnp.int32)
        pos_row = jnp.broadcast_to(pos, (tm, ROUTER_LANES)).T[0:1, :]
        for k in range(tm // chunk):
            rows = slice(k * chunk, (k + 1) * chunk)
            slot = lax.broadcasted_iota(jnp.int32, (chunk, tmp), 1).astype(F32)
            pt_ref[rows, :] = (slot == pos[rows]).astype(BF16)
        g_hi, g_lo = _split2(gate)
        d = h2_ref.shape[1]
        unsorted = jnp.concatenate([h2_ref[...], g_hi, g_lo], axis=1)
        for k in range(tmp // chunk):
            rows = slice(k * chunk, (k + 1) * chunk)
            slot = (lax.broadcasted_iota(jnp.int32, (chunk, tm), 0) + k * chunk).astype(F32)
            both = jnp.dot((slot == pos_row).astype(BF16), unsorted,
                           preferred_element_type=F32)
            hs_ref[rows, :] = both[:, 0:d].astype(BF16)
            gs_ref[rows, :] = both[:, d:d + ROUTER_LANES] + both[:, d + ROUTER_LANES:]
        ys_ref[...] = jnp.zeros_like(ys_ref)

    group = (e * MOE_EXPERTS_PER_STEP) // EXPERTS_PER_GROUP
    first = meta_ref[group]

    def block(j, _):
        rows = pl.ds(pl.multiple_of(first + j * rb, rb), rb)
        act = _gated_act(hs_ref[rows, :], wg_ref[0], wu_ref[0], gs_ref[rows, :], e)
        ys_ref[rows, :] += jnp.dot(act, wd_ref[0], preferred_element_type=F32)
        return 0

    lax.fori_loop(0, meta_ref[N_GROUPS + group], block, 0)

    @pl.when(e == pl.num_programs(1) - 1)
    def _():
        pt = pt_ref[...]
        d = y_ref.shape[1]
        for k in range(d // chunk):
            cols = slice(k * chunk, (k + 1) * chunk)
            hi, lo = _split2(ys_ref[:, cols])
            y_ref[:, cols] = (x1_ref[:, cols] + jnp.dot(pt, hi, preferred_element_type=F32)
                              + jnp.dot(pt, lo, preferred_element_type=F32))
        x2 = y_ref[...]
        ms = jnp.mean(x2 * x2, axis=-1, keepdims=True)
        y_ref[...] = x2 * lax.rsqrt(ms + EPS) * fg_ref[...]


def _moe_sorted(x1, h2, gate, wg, wu, wd, final_gain, tm):
    n, d = x1.shape
    ne, _, df = wg.shape
    rb = MOE_ROW_BLOCK
    tmp = tm + N_GROUPS * rb
    row = lambda i, e: (i, 0)
    return pl.pallas_call(
        functools.partial(_moe_sorted_kernel, rb=rb),
        grid=(n // tm, ne),
        in_specs=[
            pl.BlockSpec((tm, d), row),
            pl.BlockSpec((tm, d), row),
            pl.BlockSpec((tm, ROUTER_LANES), row),
            pl.BlockSpec((1, d, df), lambda i, e: (e, 0, 0)),
            pl.BlockSpec((1, d, df), lambda i, e: (e, 0, 0)),
            pl.BlockSpec((1, df, d), lambda i, e: (e, 0, 0)),
            pl.BlockSpec((1, d), lambda i, e: (0, 0)),
        ],
        out_specs=pl.BlockSpec((tm, d), row),
        out_shape=jax.ShapeDtypeStruct((n, d), F32),
        scratch_shapes=[
            pltpu.VMEM((tm, tmp), BF16),
            pltpu.VMEM((tmp, d), BF16),
            pltpu.VMEM((tmp, d), F32),
            pltpu.VMEM((tmp, ROUTER_LANES), F32),
            pltpu.SMEM((2 * N_GROUPS,), jnp.int32),
        ],
        compiler_params=_cparams("parallel", "arbitrary"),
        name="moe_sorted",
    )(x1, h2, gate, wg, wu, wd, final_gain.reshape(1, d))


def _tile(n, pref):
    return pref if n % pref == 0 else n


def _router_params(w_rg, b_rg, w_re, b_re):
    d = w_rg.shape[0]
    pad = ROUTER_LANES - N_GROUPS - N_EXPERTS
    w = jnp.concatenate([w_rg, w_re, jnp.zeros((d, pad), F32)], axis=1)
    hi = w.astype(BF16)
    lo = (w - hi.astype(F32)).astype(BF16)
    b = jnp.concatenate([b_rg, b_re, jnp.zeros((pad,), F32)]).reshape(1, ROUTER_LANES)
    return jnp.stack([hi, lo]), b


def _ffn(x2d, oa, ob, sb_gain, w_out_bf16, ffn_gain, w_router_hl, b_router, wg, wu, wd, final_gain):
    n = x2d.shape[0]
    x1, h2, gate = _post(x2d, oa, ob, sb_gain, w_out_bf16, ffn_gain, w_router_hl, b_router,
                         _tile(n, 512))
    if n % MOE_SORT_TILE == 0:
        return _moe_sorted(x1, h2, gate, wg, wu, wd, final_gain, MOE_SORT_TILE)
    return _moe(x1, h2, gate, wg, wu, wd, final_gain, n)


def kernel(x_prompt, x_sample, cache_k, cache_v, state_hgrn, page_table, norm_attn, w_in,
           sb_logit_bias, sb_norm, hg_lb_logits, hg_norm, w_out, norm_ffn, w_router_group,
           b_router_group, w_router_expert, b_router_expert, w_exp_gate, w_exp_up, w_exp_down,
           norm_final):
    depth = w_in.shape[0]
    assert depth == 1, "single-layer trunk"
    l = 0
    bp, t, d = x_prompt.shape
    db, tn, _ = x_sample.shape
    nh, hd = SB_HEADS, SB_HEAD_DIM
    n_pages = page_table.shape[1]

    w_in_b = w_in[l].astype(BF16)
    w_out_b = w_out[l].astype(BF16)
    wr_hl, b_r = _router_params(w_router_group[l], b_router_group[l], w_router_expert[l],
                                b_router_expert[l])
    df = w_exp_gate.shape[-1]
    per_step = MOE_EXPERTS_PER_STEP
    steps = N_EXPERTS // per_step

    def side_by_side(w):
        w = w.reshape(steps, per_step, d, df).transpose(0, 2, 1, 3)
        return w.reshape(steps, d, per_step * df).astype(BF16)

    wg = side_by_side(w_exp_gate[l])
    wu = side_by_side(w_exp_up[l])
    wd = w_exp_down[l].reshape(steps, per_step * df, d).astype(BF16)
    bias = sb_logit_bias[l].astype(F32)

    xp = x_prompt.reshape(bp * t, d)
    w3 = 3 * SB_WIDTH
    wq_scaled = w_in[l][:, :SB_WIDTH] * SB_SCALE
    wqkv_t = jnp.concatenate([wq_scaled, w_in[l][:, SB_WIDTH:w3]], axis=1).T.astype(BF16)
    ktp, vtp, qtb, kb, vtb, hgp = _inproj_prompt(
        x_prompt, norm_attn[l], wqkv_t, w_in_b[:, SB_WIDTH:2 * SB_WIDTH], w_in_b[:, w3:],
        SB_QUERY_BLOCK, SB_QUERY_BLOCK)
    oatp = _sb_prompt(qtb, kb, vtb, bias, SB_KEY_BLOCK)
    obp, sp = _hgrn_prompt(hgp, hg_lb_logits, hg_norm[l], l, 512)
    yp = _ffn(xp, oatp, obp.reshape(bp * t, HG_WIDTH), sb_norm[l], w_out_b, norm_ffn[l], wr_hl,
              b_r, wg, wu, wd, norm_final)
    kp = ktp.transpose(0, 3, 1, 2)[None]
    vp = vtp.transpose(0, 3, 1, 2)[None]

    xs = x_sample.reshape(db * tn, d)
    qs, ks, vs, hgs = _inproj(xs, norm_attn[l], w_in_b, 256)
    head_of_lane = jnp.arange(SB_WIDTH) // hd
    onehot = (head_of_lane[None, :] == jnp.arange(nh)[:, None]).astype(F32)
    qbd = (qs.reshape(db, 1, tn, SB_WIDTH) * SB_SCALE) * onehot[None, :, None, :]
    qbd = qbd.reshape(db, nh * tn, SB_WIDTH).astype(BF16)
    bias_rows = jnp.broadcast_to(jnp.repeat(bias, tn)[:, None], (nh * tn, PAGE_SIZE))
    pad = ((0, 0), (0, 8 - tn), (0, 0))
    kn = jnp.pad(ks.reshape(db, tn, SB_WIDTH), pad)
    vn = jnp.pad(vs.reshape(db, tn, SB_WIDTH), pad)
    ck = cache_k[l].transpose(0, 2, 3, 1).reshape(-1, SB_WIDTH, PAGE_SIZE)
    cv = cache_v[l].transpose(0, 2, 3, 1).reshape(-1, SB_WIDTH, PAGE_SIZE)
    o_full = _sb_sample(qbd, bias_rows, kn, vn, ck, cv, page_table.reshape(-1), n_pages, tn, 16)
    oas = jnp.sum(o_full.reshape(db, nh, tn, SB_WIDTH) * onehot[None, :, None, :], axis=1)
    obs, ss = _hgrn_sample(hgs.reshape(db, tn, -1), state_hgrn[l], hg_lb_logits, hg_norm[l], l)
    ys = _ffn(xs, oas.reshape(db * tn, SB_WIDTH), obs.reshape(db * tn, HG_WIDTH), sb_norm[l],
              w_out_b, norm_ffn[l], wr_hl, b_r, wg, wu, wd, norm_final)

    sdt = state_hgrn.dtype
    return (yp.reshape(bp, t, d), ys.reshape(db, tn, d), kp, vp, sp.astype(sdt)[None],
            ks.reshape(1, db, tn, nh, hd), vs.reshape(1, db, tn, nh, hd), ss.astype(sdt)[None])
```

```python
import functools

import jax
import jax.numpy as jnp
from jax import lax
from jax.experimental import pallas as pl
from jax.experimental.pallas import tpu as pltpu

F32 = jnp.float32
BF16 = jnp.bfloat16
EPS = 1e-6
NEG_LOG2E = -1.4426950408889634

SB_HEADS = 8
SB_HEAD_DIM = 64
SB_WIDTH = SB_HEADS * SB_HEAD_DIM
SB_SCALE = SB_HEAD_DIM ** -0.5
HG_HEADS = 4
HG_DK = 128
HG_DV = 128
HG_WIDTH = HG_HEADS * HG_DK
HG_CHUNK = 64
HG_SUB = 16
N_GROUPS = 4
EXPERTS_PER_GROUP = 8
N_EXPERTS = N_GROUPS * EXPERTS_PER_GROUP
ROUTER_LANES = 128
GROUP_ID_LANE = ROUTER_LANES - 1
MOE_ROW_BLOCK = 256
MOE_EXPERTS_PER_STEP = 2
MOE_SORT_TILE = 1024
PAGE_SIZE = 128
SB_KEY_BLOCK = 128
SB_QUERY_BLOCK = 512

VMEM_LIMIT_BYTES = 56 * 1024 * 1024

NT_DIMS = (((1,), (1,)), ((), ()))
TN_DIMS = (((0,), (0,)), ((), ()))


def _cparams(*sem):
    return pltpu.CompilerParams(dimension_semantics=sem, vmem_limit_bytes=VMEM_LIMIT_BYTES)


def _split2(x):
    hi = x.astype(BF16)
    lo = (x - hi.astype(F32)).astype(BF16)
    return hi, lo


def _split3(x):
    hi = x.astype(BF16)
    r = x - hi.astype(F32)
    mid = r.astype(BF16)
    lo = (r - mid.astype(F32)).astype(BF16)
    return hi, mid, lo


def _softplus(z):
    return jnp.maximum(z, 0.0) + jnp.log(1.0 + jnp.exp2(jnp.abs(z) * NEG_LOG2E))


def _sigmoid(x):
    return 1.0 / (1.0 + jnp.exp(-x))


def _inproj_kernel(x_ref, g_ref, w_ref, q_ref, k_ref, v_ref, hg_ref):
    x = x_ref[...]
    ms = jnp.mean(x * x, axis=-1, keepdims=True)
    h = (x * lax.rsqrt(ms + EPS) * g_ref[...]).astype(BF16)
    w = SB_WIDTH
    q_ref[...] = jnp.dot(h, w_ref[:, 0:w], preferred_element_type=F32)
    k_ref[...] = jnp.dot(h, w_ref[:, w:2 * w], preferred_element_type=F32)
    v_ref[...] = jnp.dot(h, w_ref[:, 2 * w:3 * w], preferred_element_type=F32)
    hg_ref[...] = jnp.dot(h, w_ref[:, 3 * w:], preferred_element_type=F32)


def _inproj(x2d, gain, w_in_bf16, tm):
    n, d = x2d.shape
    e = w_in_bf16.shape[1]
    hgw = e - 3 * SB_WIDTH
    return pl.pallas_call(
        _inproj_kernel,
        grid=(n // tm,),
        in_specs=[
            pl.BlockSpec((tm, d), lambda i: (i, 0)),
            pl.BlockSpec((1, d), lambda i: (0, 0)),
            pl.BlockSpec((d, e), lambda i: (0, 0)),
        ],
        out_specs=[
            pl.BlockSpec((tm, SB_WIDTH), lambda i: (i, 0)),
            pl.BlockSpec((tm, SB_WIDTH), lambda i: (i, 0)),
            pl.BlockSpec((tm, SB_WIDTH), lambda i: (i, 0)),
            pl.BlockSpec((tm, hgw), lambda i: (i, 0)),
        ],
        out_shape=[
            jax.ShapeDtypeStruct((n, SB_WIDTH), F32),
            jax.ShapeDtypeStruct((n, SB_WIDTH), F32),
            jax.ShapeDtypeStruct((n, SB_WIDTH), F32),
            jax.ShapeDtypeStruct((n, hgw), F32),
        ],
        compiler_params=_cparams("parallel"),
        name="inproj",
    )(x2d, gain.reshape(1, d), w_in_bf16)


def _inproj_prompt_kernel(x_ref, g_ref, wt_ref, wk_ref, whg_ref,
                          kt_ref, vt_ref, qtb_ref, kb_ref, vtb_ref, hg_ref, *, tk):
    x = x_ref[0]
    tm = x.shape[0]
    ms = jnp.mean(x * x, axis=-1, keepdims=True)
    h = (x * lax.rsqrt(ms + EPS) * g_ref[...]).astype(BF16)
    qkvt = lax.dot_general(wt_ref[...], h, NT_DIMS, preferred_element_type=F32)
    k = jnp.dot(h, wk_ref[...], preferred_element_type=F32)
    w, hd = SB_WIDTH, SB_HEAD_DIM
    for hh in range(SB_HEADS):
        r = hh * hd
        qtb_ref[0, hh] = qkvt[r:r + hd].astype(BF16)
        kt_ref[0, hh] = qkvt[w + r:w + r + hd]
        vt = qkvt[2 * w + r:2 * w + r + hd]
        vt_ref[0, hh] = vt
        for c in range(tm // tk):
            vtb_ref[0, hh, c] = vt[:, c * tk:(c + 1) * tk].astype(BF16)
        kb_ref[0, hh] = k[:, r:r + hd].astype(BF16)
    hg_ref[0] = jnp.dot(h, whg_ref[...], preferred_element_type=F32)


def _inproj_prompt(x, gain, wqkv_t, wk, whg, tm, tk):
    b, t, d = x.shape
    nh, hd, w = SB_HEADS, SB_HEAD_DIM, SB_WIDTH
    hgw = whg.shape[1]
    fixed = lambda bi, i: (0, 0)
    tspec = pl.BlockSpec((1, nh, hd, tm), lambda bi, i: (bi, 0, 0, i))
    return pl.pallas_call(
        functools.partial(_inproj_prompt_kernel, tk=tk),
        grid=(b, t // tm),
        in_specs=[
            pl.BlockSpec((1, tm, d), lambda bi, i: (bi, i, 0)),
            pl.BlockSpec((1, d), fixed),
            pl.BlockSpec((3 * w, d), fixed),
            pl.BlockSpec((d, w), fixed),
            pl.BlockSpec((d, hgw), fixed),
        ],
        out_specs=[
            tspec, tspec, tspec,
            pl.BlockSpec((1, nh, tm, hd), lambda bi, i: (bi, 0, i, 0)),
            pl.BlockSpec((1, nh, tm // tk, hd, tk), lambda bi, i: (bi, 0, i, 0, 0)),
            pl.BlockSpec((1, tm, hgw), lambda bi, i: (bi, i, 0)),
        ],
        out_shape=[
            jax.ShapeDtypeStruct((b, nh, hd, t), F32),
            jax.ShapeDtypeStruct((b, nh, hd, t), F32),
            jax.ShapeDtypeStruct((b, nh, hd, t), BF16),
            jax.ShapeDtypeStruct((b, nh, t, hd), BF16),
            jax.ShapeDtypeStruct((b, nh, t // tk, hd, tk), BF16),
            jax.ShapeDtypeStruct((b, t, hgw), F32),
        ],
        compiler_params=_cparams("parallel", "parallel"),
        name="inproj_prompt",
    )(x, gain.reshape(1, d), wqkv_t, wk, whg)


def _sb_row_weights(z, tri2, carry, n, before=None):
    sp = _softplus(z)
    if before is not None:
        sp = jnp.where(before, sp, 0.0)
        z = jnp.where(before, z, -jnp.inf)
    hi, lo = _split2(sp)
    blocks = []
    for p in range(z.shape[1] // n):
        sl = slice(p * n, (p + 1) * n)
        cs = jnp.dot(jnp.concatenate([hi[:, sl], lo[:, sl]], axis=1), tri2,
                     preferred_element_type=F32)
        blocks.append(jnp.exp(z[:, sl] - cs - carry).astype(BF16))
        carry = carry + cs[:, 0:1]
    return jnp.concatenate(blocks, axis=1), carry


def _tri_incl(n):
    j = lax.broadcasted_iota(jnp.int32, (n, n), 0)
    s = lax.broadcasted_iota(jnp.int32, (n, n), 1)
    return (j >= s).astype(BF16)


def _sb_prompt_kernel(bias_ref, qt_ref, k_ref, vt_ref, ot_ref, z0_ref, hl0_ref, z1_ref, hl1_ref,
                      *, tq, tk):
    h = pl.program_id(1)
    i = pl.program_id(2)
    bias = bias_ref[h]
    qt = qt_ref[0, 0]
    nb = tq // tk
    r = lax.broadcasted_iota(jnp.int32, (tk, tk), 0)
    c = lax.broadcasted_iota(jnp.int32, (tk, tk), 1)
    triu = (c >= r).astype(BF16)
    tri2 = jnp.concatenate([triu, triu], axis=1)
    bufs = ((z0_ref, hl0_ref), (z1_ref, hl1_ref))

    def scores(g, slot, masked):
        z_ref, hl_ref = bufs[slot]
        rows = k_ref[0, 0, pl.ds(pl.multiple_of(g * tq, tq), tq), :]
        z = jnp.dot(rows, qt, preferred_element_type=F32) + bias
        sp = _softplus(z)
        if masked:
            s_loc = lax.broadcasted_iota(jnp.int32, (tq, tq), 0)
            t_loc = lax.broadcasted_iota(jnp.int32, (tq, tq), 1)
            before = s_loc < t_loc
            sp = jnp.where(before, sp, 0.0)
            z = jnp.where(before, z, -jnp.inf)
        z_ref[...] = z
        hi, lo = _split2(sp)
        for p in range(nb):
            hl_ref[p, 0:tk, :] = hi[p * tk:(p + 1) * tk]
            hl_ref[p, tk:2 * tk, :] = lo[p * tk:(p + 1) * tk]

    def weights(g, slot, carry, acc):
        z_ref, hl_ref = bufs[slot]
        a_blocks = [None] * nb
        for p in reversed(range(nb)):
            cs = jnp.dot(tri2, hl_ref[p], preferred_element_type=F32)
            la = z_ref[p * tk:(p + 1) * tk, :] - cs - carry
            a_blocks[p] = jnp.exp(la).astype(BF16)
            carry = carry + cs[0:1, :]
        a = jnp.concatenate(a_blocks, axis=0)
        return carry, acc + jnp.dot(vt_ref[0, 0, g], a, preferred_element_type=F32)

    scores(i, 0, True)

    def pair(m, c):
        g = i - 2 * m
        scores(g - 1, 1, False)
        c = weights(g, 0, *c)
        scores(g - 2, 0, False)
        return weights(g - 1, 1, *c)

    init = (jnp.zeros((1, tq), F32), jnp.zeros((SB_HEAD_DIM, tq), F32))
    c = lax.fori_loop(0, i // 2, pair, init)

    def odd_tail(c):
        scores(0, 1, False)
        return weights(0, 1, *weights(1, 0, *c))

    carry, acc = lax.cond(i % 2 == 1, odd_tail, lambda c: weights(0, 0, *c), c)
    ot_ref[0, 0] = acc


def _sb_prompt(qt, k, vt, bias, tk):
    b, h, d, t = qt.shape
    tq = vt.shape[-1]
    return pl.pallas_call(
        functools.partial(_sb_prompt_kernel, tq=tq, tk=tk),
        grid=(b, h, t // tq),
        in_specs=[
            pl.BlockSpec(memory_space=pltpu.SMEM),
            pl.BlockSpec((1, 1, d, tq), lambda bi, hi, i: (bi, hi, 0, i)),
            pl.BlockSpec((1, 1, t, d), lambda bi, hi, i: (bi, hi, 0, 0)),
            pl.BlockSpec((1, 1, t // tq, d, tq), lambda bi, hi, i: (bi, hi, 0, 0, 0)),
        ],
        out_specs=pl.BlockSpec((1, 1, d, tq), lambda bi, hi, i: (bi, hi, 0, i)),
        out_shape=jax.ShapeDtypeStruct((b, h, d, t), F32),
        scratch_shapes=[
            pltpu.VMEM((tq, tq), F32),
            pltpu.VMEM((tq // tk, 2 * tk, tq), BF16),
        ] * 2,
        compiler_params=_cparams("parallel", "parallel", "arbitrary"),
        name="sb_prompt",
    )(bias, qt, k, vt)


def _sb_sample_kernel(pt_ref, qbd_ref, bias_ref, kn_ref, vn_ref, *rest, pages_per_step, t_new):
    g = pages_per_step
    k_refs = rest[:g]
    v_refs = rest[g:2 * g]
    o_ref = rest[2 * g]
    carry_ref, acc_ref, kpad_ref, vpad_ref = rest[2 * g + 1:]
    step_id = pl.program_id(1)
    rows = qbd_ref.shape[1]
    n = PAGE_SIZE
    qbd = qbd_ref[0]
    bias = bias_ref[...]
    tri = _tri_incl(n)
    tri2 = jnp.concatenate([tri, tri], axis=0)

    @pl.when(step_id == 0)
    def _():
        kpad_ref[...] = jnp.zeros_like(kpad_ref)
        vpad_ref[...] = jnp.zeros_like(vpad_ref)
        kpad_ref[0:kn_ref.shape[1], :] = kn_ref[0]
        vpad_ref[0:vn_ref.shape[1], :] = vn_ref[0]
        qi = lax.broadcasted_iota(jnp.int32, (rows, n), 0) % t_new
        s = lax.broadcasted_iota(jnp.int32, (rows, n), 1)
        z = lax.dot_general(qbd, kpad_ref[...].astype(BF16), NT_DIMS,
                            preferred_element_type=F32) + bias
        a, carry = _sb_row_weights(z, tri2, jnp.zeros((rows, 1), F32), n, s < qi)
        carry_ref[...] = jnp.broadcast_to(carry, carry_ref.shape)
        acc_ref[...] = jnp.dot(a, vpad_ref[...].astype(BF16), preferred_element_type=F32)

    kcat = jnp.concatenate([k_refs[p][0].astype(BF16) for p in range(g)], axis=1)
    vcat = jnp.concatenate([v_refs[p][0].astype(BF16) for p in range(g)], axis=1)
    z = jnp.dot(qbd, kcat, preferred_element_type=F32) + jnp.tile(bias, (1, g))
    a, carry = _sb_row_weights(z, tri2, carry_ref[:, 0:1], n)
    acc = acc_ref[...] + lax.dot_general(a, vcat, NT_DIMS, preferred_element_type=F32)
    carry_ref[...] = jnp.broadcast_to(carry, carry_ref.shape)
    acc_ref[...] = acc

    @pl.when(step_id == pl.num_programs(1) - 1)
    def _():
        o_ref[0] = acc


def _sb_sample(qbd, bias_rows, k_new, v_new, cache_k, cache_v, page_table_flat, n_pages, t_new,
               pages_per_step):
    b, rows, w = qbd.shape
    g = pages_per_step
    steps = n_pages // g

    def page_map(p):
        return lambda bi, si, pt: (pt[bi * n_pages + (n_pages - 1 - (si * g + p))], 0, 0)

    page_specs = [pl.BlockSpec((1, w, PAGE_SIZE), page_map(p)) for p in range(g)]
    grid_spec = pltpu.PrefetchScalarGridSpec(
        num_scalar_prefetch=1,
        grid=(b, steps),
        in_specs=[
            pl.BlockSpec((1, rows, w), lambda bi, si, pt: (bi, 0, 0)),
            pl.BlockSpec((rows, PAGE_SIZE), lambda bi, si, pt: (0, 0)),
            pl.BlockSpec((1, k_new.shape[1], w), lambda bi, si, pt: (bi, 0, 0)),
            pl.BlockSpec((1, v_new.shape[1], w), lambda bi, si, pt: (bi, 0, 0)),
        ] + page_specs + page_specs,
        out_specs=pl.BlockSpec((1, rows, w), lambda bi, si, pt: (bi, 0, 0)),
        scratch_shapes=[
            pltpu.VMEM((rows, PAGE_SIZE), F32),
            pltpu.VMEM((rows, w), F32),
            pltpu.VMEM((PAGE_SIZE, w), F32),
            pltpu.VMEM((PAGE_SIZE, w), F32),
        ],
    )
    return pl.pallas_call(
        functools.partial(_sb_sample_kernel, pages_per_step=g, t_new=t_new),
        grid_spec=grid_spec,
        out_shape=jax.ShapeDtypeStruct((b, rows, w), F32),
        compiler_params=_cparams("parallel", "arbitrary"),
        name="sb_sample",
    )(page_table_flat, qbd, bias_rows, k_new, v_new, *([cache_k] * g), *([cache_v] * g))


def _lower_bound(lb_logits, layer):
    m = jnp.max(lb_logits, axis=0, keepdims=True)
    e = jnp.exp(lb_logits - m)
    return jnp.sum(e[0:layer + 1], axis=0, keepdims=True) / jnp.sum(e, axis=0, keepdims=True)


def _hg_out(o, gain, gb):
    ms = jnp.mean(o * o, axis=-1, keepdims=True)
    return (o * lax.rsqrt(ms + EPS) * gain) * (gb * _sigmoid(gb))


def _hgrn_prompt_kernel(lbl_ref, gain_ref, hg_ref, o_ref, s_ref, st_ref, *, layer, n_chunks):
    c = HG_CHUNK
    w = HG_WIDTH
    half = HG_SUB // 2
    heads = range(HG_HEADS)
    t_id = pl.program_id(1)

    @pl.when(t_id == 0)
    def _():
        st_ref[...] = jnp.zeros_like(st_ref)

    lb_all = _lower_bound(lbl_ref[...], layer)
    gain = gain_ref[...]
    ti = lax.broadcasted_iota(jnp.int32, (c, c), 0)
    si = lax.broadcasted_iota(jnp.int32, (c, c), 1)
    ltri = (ti >= si).astype(BF16)
    ltri3 = jnp.concatenate([ltri, ltri, ltri], axis=1)
    rowc = lax.broadcasted_iota(jnp.int32, (c, HG_DK), 0)
    rowh = lax.broadcasted_iota(jnp.int32, (half, HG_DK), 0)

    def direct(g_q, q_q, g_k, k_k, v_k, causal):
        od = jnp.zeros((half, HG_DV), F32)
        for s in range(half):
            d = g_q - g_k[s:s + 1, :]
            if causal:
                d = jnp.where(rowh >= s, d, -jnp.inf)
            a = jnp.sum(q_q * (k_k[s:s + 1, :] * jnp.exp(d)), axis=1, keepdims=True)
            od = od + a * v_k[s:s + 1, :]
        return od

    def chunk(n, _):
        sl = pl.ds(pl.multiple_of(n * c, c), c)

        def col(which, h):
            lo = which * w + h * HG_DK
            return hg_ref[0, sl, lo:lo + HG_DK]

        f = [lb_all[:, h * HG_DK:(h + 1) * HG_DK]
             + (1.0 - lb_all[:, h * HG_DK:(h + 1) * HG_DK]) * _sigmoid(col(1, h)) for h in heads]
        kk = [1.0 - f[h] for h in heads]
        qq = [col(0, h) * _sigmoid(col(0, h)) for h in heads]
        v = [col(2, h) for h in heads]
        vb = [v[h].astype(BF16) for h in heads]
        gcum = [jnp.dot(ltri3, jnp.concatenate(_split3(jnp.log(f[h])), axis=0),
                        preferred_element_type=F32) for h in heads]
        g_last = [gcum[h][c - 1:c, :] for h in heads]
        st = [st_ref[h] for h in heads]
        o = [lax.dot_general((qq[h] * jnp.exp(gcum[h])).astype(BF16), st[h].astype(BF16),
                             NT_DIMS, preferred_element_type=F32) for h in heads]
        parts = [[jnp.zeros((HG_SUB, HG_DV), F32)] for _ in heads]
        for i in range(1, c // HG_SUB):
            lo = i * HG_SUB
            att = []
            for h in heads:
                r = gcum[h][lo - 1:lo, :]
                qh = qq[h][lo:lo + HG_SUB] * jnp.exp(gcum[h][lo:lo + HG_SUB] - r)
                kh = kk[h] * jnp.exp(jnp.where(rowc < lo, r - gcum[h], -jnp.inf))
                att.append(lax.dot_general(qh.astype(BF16), kh.astype(BF16), NT_DIMS,
                                           preferred_element_type=F32))
            for h in heads:
                parts[h].append(jnp.dot(att[h].astype(BF16), vb[h], preferred_element_type=F32))
        for h in heads:
            rows_out = []
            for i in range(c // HG_SUB):
                top = slice(i * HG_SUB, i * HG_SUB + half)
                bot = slice(i * HG_SUB + half, (i + 1) * HG_SUB)
                g, q, k, vv = gcum[h], qq[h], kk[h], v[h]
                od_top = direct(g[top], q[top], g[top], k[top], vv[top], True)
                od_bot = (direct(g[bot], q[bot], g[top], k[top], vv[top], False)
                          + direct(g[bot], q[bot], g[bot], k[bot], vv[bot], True))
                rows_out.append(jnp.concatenate([od_top, od_bot], axis=0) + parts[h][i])
            oh = o[h] + jnp.concatenate(rows_out, axis=0)
            o_ref[0, sl, h * HG_DV:(h + 1) * HG_DV] = _hg_out(oh, gain, col(3, h))
        for h in heads:
            ke = kk[h] * jnp.exp(g_last[h] - gcum[h])
            st_ref[h] = st[h] * jnp.exp(g_last[h]) + lax.dot_general(
                vb[h], ke.astype(BF16), TN_DIMS, preferred_element_type=F32)
        return 0

    lax.fori_loop(0, n_chunks, chunk, 0)

    @pl.when(t_id == pl.num_programs(1) - 1)
    def _():
        for h in heads:
            s_ref[0, h] = st_ref[h].T


def _hgrn_prompt(hg, lb_logits, gain, layer, tt):
    b, t, e = hg.shape
    nh = HG_HEADS
    return pl.pallas_call(
        functools.partial(_hgrn_prompt_kernel, layer=layer, n_chunks=tt // HG_CHUNK),
        grid=(b, t // tt),
        in_specs=[
            pl.BlockSpec(lb_logits.shape, lambda bi, ti: (0, 0)),
            pl.BlockSpec((1, HG_DV), lambda bi, ti: (0, 0)),
            pl.BlockSpec((1, tt, e), lambda bi, ti: (bi, ti, 0)),
        ],
        out_specs=[
            pl.BlockSpec((1, tt, HG_WIDTH), lambda bi, ti: (bi, ti, 0)),
            pl.BlockSpec((1, nh, HG_DK, HG_DV), lambda bi, ti: (bi, 0, 0, 0)),
        ],
        out_shape=[
            jax.ShapeDtypeStruct((b, t, HG_WIDTH), F32),
            jax.ShapeDtypeStruct((b, nh, HG_DK, HG_DV), F32),
        ],
        scratch_shapes=[pltpu.VMEM((nh, HG_DV, HG_DK), F32)],
        compiler_params=_cparams("parallel", "arbitrary"),
        name="hgrn_prompt",
    )(lb_logits, gain.reshape(1, HG_DV), hg)


def _hgrn_sample_kernel(lbl_ref, gain_ref, hg_ref, s0_ref, o_ref, s_ref, *, layer, t_new):
    w = HG_WIDTH
    mxu_rows = 16
    lb_all = _lower_bound(lbl_ref[...], layer)
    gain = gain_ref[...]
    x = hg_ref[0]
    rowt = lax.broadcasted_iota(jnp.int32, (t_new, HG_DK), 0)

    def padded(a):
        return jnp.concatenate([a, jnp.zeros((mxu_rows - t_new, a.shape[1]), F32)], axis=0)

    outs = []
    for h in range(HG_HEADS):
        c0 = h * HG_DK
        lb = lb_all[:, c0:c0 + HG_DK]
        qx = x[:, c0:c0 + HG_DK]
        qq = qx * _sigmoid(qx)
        f = lb + (1.0 - lb) * _sigmoid(x[:, w + c0:w + c0 + HG_DK])
        kk = 1.0 - f
        v = x[:, 2 * w + c0:2 * w + c0 + HG_DV]
        gb = x[:, 3 * w + c0:3 * w + c0 + HG_DV]
        logf = jnp.log(f)
        g_rows = [logf[0:1, :]]
        for t in range(1, t_new):
            g_rows.append(g_rows[-1] + logf[t:t + 1, :])
        gcum = jnp.concatenate(g_rows, axis=0)
        g_last = g_rows[-1]
        s0 = s0_ref[0, h]
        o = jnp.dot(padded(qq * jnp.exp(gcum)).astype(BF16), s0.astype(BF16),
                    preferred_element_type=F32)[0:t_new]
        for s in range(t_new):
            d = jnp.where(rowt >= s, gcum - g_rows[s], -jnp.inf)
            a = jnp.sum(qq * (kk[s:s + 1, :] * jnp.exp(d)), axis=1, keepdims=True)
            o = o + a * v[s:s + 1, :]
        ke = padded(kk * jnp.exp(g_last - gcum)).astype(BF16)
        outer = lax.dot_general(ke, padded(v).astype(BF16), TN_DIMS, preferred_element_type=F32)
        e_rows = jnp.concatenate([jnp.exp(g_last), jnp.zeros((HG_DK - 1, HG_DK), F32)], axis=0)
        s_ref[0, h] = e_rows.T[:, 0:1] * s0 + outer
        outs.append(_hg_out(o, gain, gb))
    o_ref[0] = jnp.concatenate(outs, axis=1)


def _hgrn_sample(hg, state, lb_logits, gain, layer):
    b, t_new, e = hg.shape
    nh = HG_HEADS
    return pl.pallas_call(
        functools.partial(_hgrn_sample_kernel, layer=layer, t_new=t_new),
        grid=(b,),
        in_specs=[
            pl.BlockSpec(lb_logits.shape, lambda bi: (0, 0)),
            pl.BlockSpec((1, HG_DV), lambda bi: (0, 0)),
            pl.BlockSpec((1, t_new, e), lambda bi: (bi, 0, 0)),
            pl.BlockSpec((1, nh, HG_DK, HG_DV), lambda bi: (bi, 0, 0, 0)),
        ],
        out_specs=[
            pl.BlockSpec((1, t_new, HG_WIDTH), lambda bi: (bi, 0, 0)),
            pl.BlockSpec((1, nh, HG_DK, HG_DV), lambda bi: (bi, 0, 0, 0)),
        ],
        out_shape=[
            jax.ShapeDtypeStruct((b, t_new, HG_WIDTH), F32),
            jax.ShapeDtypeStruct((b, nh, HG_DK, HG_DV), F32),
        ],
        compiler_params=_cparams("parallel"),
        name="hgrn_sample",
    )(lb_logits, gain.reshape(1, HG_DV), hg, state)


def _route(logits):
    lane = lax.broadcasted_iota(jnp.int32, logits.shape, 1)
    neg = -jnp.inf
    big = ROUTER_LANES

    def top(mask):
        val = jnp.max(jnp.where(mask, logits, neg), axis=1, keepdims=True)
        idx = jnp.min(jnp.where(mask & (logits == val), lane, big), axis=1, keepdims=True)
        return val, idx

    is_group = lane < N_GROUPS
    gmax, gidx = top(is_group)
    p_top = 1.0 / jnp.sum(jnp.where(is_group, jnp.exp(logits - gmax), 0.0), axis=1, keepdims=True)
    lo = N_GROUPS + gidx * EXPERTS_PER_GROUP
    in_group = (lane >= lo) & (lane < lo + EXPERTS_PER_GROUP)
    v1, i1 = top(in_group)
    v2, i2 = top(in_group & (lane != i1))
    e2 = jnp.exp(v2 - v1)
    w1 = p_top / (1.0 + e2)
    w2 = p_top * e2 / (1.0 + e2)
    gate = jnp.where(lane == i1, w1, 0.0) + jnp.where(lane == i2, w2, 0.0)
    return jnp.where(lane == GROUP_ID_LANE, gidx.astype(F32), gate)


def _post_kernel(x_ref, oa_ref, ob_ref, sbg_ref, wo_ref, ng_ref, wr_ref, br_ref,
                 x1_ref, h2_ref, gate_ref, *, oa_transposed):
    w = SB_WIDTH
    if oa_transposed:
        oa = jnp.concatenate([oa_ref[0, hh] for hh in range(SB_HEADS)], axis=0)
        ms = jnp.mean(oa * oa, axis=0, keepdims=True)
        oa = (oa * lax.rsqrt(ms + EPS) * sbg_ref[...]).astype(BF16)
        mixed = lax.dot_general(oa, wo_ref[0:w, :], TN_DIMS, preferred_element_type=F32)
    else:
        oa = oa_ref[...]
        ms = jnp.mean(oa * oa, axis=-1, keepdims=True)
        oa = (oa * lax.rsqrt(ms + EPS) * sbg_ref[...]).astype(BF16)
        mixed = jnp.dot(oa, wo_ref[0:w, :], preferred_element_type=F32)
    mixed = mixed + jnp.dot(ob_ref[...].astype(BF16), wo_ref[w:, :], preferred_element_type=F32)
    x1 = x_ref[...] + mixed
    x1_ref[...] = x1
    ms = jnp.mean(x1 * x1, axis=-1, keepdims=True)
    h2 = x1 * lax.rsqrt(ms + EPS) * ng_ref[...]
    h2_ref[...] = h2.astype(BF16)
    hh, hl = _split2(h2)
    logits = (jnp.dot(hh, wr_ref[0], preferred_element_type=F32)
              + jnp.dot(hl, wr_ref[0], preferred_element_type=F32)
              + jnp.dot(hh, wr_ref[1], preferred_element_type=F32)) + br_ref[...]
    gate_ref[...] = _route(logits)


def _post(x2d, oa, ob, sb_gain, w_out_bf16, ffn_gain, w_router_hl, b_router, tm):
    n, d = x2d.shape
    row = lambda i: (i, 0)
    fixed = lambda i: (0, 0)
    oa_transposed = oa.ndim == 4
    if oa_transposed:
        per_b = oa.shape[3] // tm
        oa_spec = pl.BlockSpec((1, SB_HEADS, SB_HEAD_DIM, tm),
                               lambda i: (i // per_b, 0, 0, i % per_b))
        sb_gain = sb_gain.reshape(SB_WIDTH, 1)
    else:
        oa_spec = pl.BlockSpec((tm, SB_WIDTH), row)
        sb_gain = sb_gain.reshape(1, SB_WIDTH)
    return pl.pallas_call(
        functools.partial(_post_kernel, oa_transposed=oa_transposed),
        grid=(n // tm,),
        in_specs=[
            pl.BlockSpec((tm, d), row),
            oa_spec,
            pl.BlockSpec((tm, HG_WIDTH), row),
            pl.BlockSpec(sb_gain.shape, fixed),
            pl.BlockSpec(w_out_bf16.shape, fixed),
            pl.BlockSpec((1, d), fixed),
            pl.BlockSpec(w_router_hl.shape, lambda i: (0, 0, 0)),
            pl.BlockSpec((1, ROUTER_LANES), fixed),
        ],
        out_specs=[
            pl.BlockSpec((tm, d), row),
            pl.BlockSpec((tm, d), row),
            pl.BlockSpec((tm, ROUTER_LANES), row),
        ],
        out_shape=[
            jax.ShapeDtypeStruct((n, d), F32),
            jax.ShapeDtypeStruct((n, d), BF16),
            jax.ShapeDtypeStruct((n, ROUTER_LANES), F32),
        ],
        compiler_params=_cparams("parallel"),
        name="post",
    )(x2d, oa, ob, sb_gain, w_out_bf16, ffn_gain.reshape(1, d), w_router_hl, b_router)


def _experts(h, wg_ref, wu_ref, wd_ref, gates, step):
    lane = lax.broadcasted_iota(jnp.int32, gates.shape, 1)
    out = None
    for k in range(wg_ref.shape[0]):
        e = step * wg_ref.shape[0] + k
        gcol = jnp.sum(jnp.where(lane == N_GROUPS + e, gates, 0.0), axis=1, keepdims=True)
        a = jnp.dot(h, wg_ref[k], preferred_element_type=F32)
        u = jnp.dot(h, wu_ref[k], preferred_element_type=F32)
        act = ((a * _sigmoid(a)) * u * gcol).astype(BF16)
        y = jnp.dot(act, wd_ref[k], preferred_element_type=F32)
        out = y if out is None else out + y
    return out


def _moe_kernel(x1_ref, h2_ref, gate_ref, wg_ref, wu_ref, wd_ref, fg_ref, y_ref, acc_ref):
    e = pl.program_id(1)

    @pl.when(e == 0)
    def _():
        acc_ref[...] = jnp.zeros_like(acc_ref)

    acc_ref[...] += _experts(h2_ref[...], wg_ref, wu_ref, wd_ref, gate_ref[...], e)

    @pl.when(e == pl.num_programs(1) - 1)
    def _():
        x2 = x1_ref[...] + acc_ref[...]
        ms = jnp.mean(x2 * x2, axis=-1, keepdims=True)
        y_ref[...] = x2 * lax.rsqrt(ms + EPS) * fg_ref[...]


def _moe(x1, h2, gate, wg, wu, wd, final_gain, tm):
    n, d = x1.shape
    ne, _, df = wg.shape
    row = lambda i, e: (i, 0)
    return pl.pallas_call(
        _moe_kernel,
        grid=(n // tm, ne // MOE_EXPERTS_PER_STEP),
        in_specs=[
            pl.BlockSpec((tm, d), row),
            pl.BlockSpec((tm, d), row),
            pl.BlockSpec((tm, ROUTER_LANES), row),
            pl.BlockSpec((MOE_EXPERTS_PER_STEP, d, df), lambda i, e: (e, 0, 0)),
            pl.BlockSpec((MOE_EXPERTS_PER_STEP, d, df), lambda i, e: (e, 0, 0)),
            pl.BlockSpec((MOE_EXPERTS_PER_STEP, df, d), lambda i, e: (e, 0, 0)),
            pl.BlockSpec((1, d), lambda i, e: (0, 0)),
        ],
        out_specs=pl.BlockSpec((tm, d), row),
        out_shape=jax.ShapeDtypeStruct((n, d), F32),
        scratch_shapes=[pltpu.VMEM((tm, d), F32)],
        compiler_params=_cparams("parallel", "arbitrary"),
        name="moe",
    )(x1, h2, gate, wg, wu, wd, final_gain.reshape(1, d))


def _moe_sorted_kernel(x1_ref, h2_ref, gate_ref, wg_ref, wu_ref, wd_ref, fg_ref, y_ref,
                       pt_ref, hs_ref, ys_ref, gs_ref, meta_ref, *, rb):
    e = pl.program_id(1)
    tm = h2_ref.shape[0]
    tmp = hs_ref.shape[0]
    chunk = 256

    @pl.when(e == 0)
    def _():
        gate = gate_ref[...]
        lane = lax.broadcasted_iota(jnp.int32, gate.shape, 1)
        onehot = lane.astype(F32) == gate[:, GROUP_ID_LANE:GROUP_ID_LANE + 1]
        r = lax.broadcasted_iota(jnp.int32, (tm, tm), 0)
        c = lax.broadcasted_iota(jnp.int32, (tm, tm), 1)
        earlier = jnp.dot((c < r).astype(BF16), onehot.astype(BF16),
                          preferred_element_type=F32)
        count = earlier[tm - 1:tm, :] + onehot[tm - 1:tm, :].astype(F32)
        padded = jnp.ceil(count * (1.0 / rb)) * rb
        g0 = lax.broadcasted_iota(jnp.int32, (ROUTER_LANES, ROUTER_LANES), 0)
        g1 = lax.broadcasted_iota(jnp.int32, (ROUTER_LANES, ROUTER_LANES), 1)
        start = jnp.dot(jnp.broadcast_to(padded, (8, ROUTER_LANES)).astype(BF16),
                        (g0 < g1).astype(BF16), preferred_element_type=F32)[0:1, :]
        pos = jnp.sum(jnp.where(onehot, earlier + start, 0.0), axis=1, keepdims=True)
        lane1 = lax.broadcasted_iota(jnp.int32, (1, ROUTER_LANES), 1)
        for g in range(N_GROUPS):
            meta_ref[g] = jnp.sum(jnp.where(lane1 == g, start, 0.0)).astype(jnp.int32)
            meta_ref[N_GROUPS + g] = jnp.sum(
                jnp.where(lane1 == g, padded * (1.0 / rb), 0.0)).astype(jnp.int32)
        pos_row = jnp.broadcast_to(pos, (tm, ROUTER_LANES)).T[0:1, :]
        for k in range(tm // chunk):
            rows = slice(k * chunk, (k + 1) * chunk)
            slot = lax.broadcasted_iota(jnp.int32, (chunk, tmp), 1).astype(F32)
            pt_ref[rows, :] = (slot == pos[rows]).astype(BF16)
        g_hi, g_lo = _split2(gate)
        d = h2_ref.shape[1]
        unsorted = jnp.concatenate([h2_ref[...], g_hi, g_lo], axis=1)
        for k in range(tmp // chunk):
            rows = slice(k * chunk, (k + 1) * chunk)
            slot = (lax.broadcasted_iota(jnp.int32, (chunk, tm), 0) + k * chunk).astype(F32)
            both = jnp.dot((slot == pos_row).astype(BF16), unsorted,
                           preferred_element_type=F32)
            hs_ref[rows, :] = both[:, 0:d].astype(BF16)
            gs_ref[rows, :] = both[:, d:d + ROUTER_LANES] + both[:, d + ROUTER_LANES:]
        ys_ref[...] = jnp.zeros_like(ys_ref)

    group = (e * MOE_EXPERTS_PER_STEP) // EXPERTS_PER_GROUP
    first = meta_ref[group]

    def block(j, _):
        rows = pl.ds(pl.multiple_of(first + j * rb, rb), rb)
        ys_ref[rows, :] += _experts(hs_ref[rows, :], wg_ref, wu_ref, wd_ref, gs_ref[rows, :], e)
        return 0

    lax.fori_loop(0, meta_ref[N_GROUPS + group], block, 0)

    @pl.when(e == pl.num_programs(1) - 1)
    def _():
        pt = pt_ref[...]
        d = y_ref.shape[1]
        for k in range(d // chunk):
            cols = slice(k * chunk, (k + 1) * chunk)
            hi, lo = _split2(ys_ref[:, cols])
            y_ref[:, cols] = (x1_ref[:, cols] + jnp.dot(pt, hi, preferred_element_type=F32)
                              + jnp.dot(pt, lo, preferred_element_type=F32))
        x2 = y_ref[...]
        ms = jnp.mean(x2 * x2, axis=-1, keepdims=True)
        y_ref[...] = x2 * lax.rsqrt(ms + EPS) * fg_ref[...]


def _moe_sorted(x1, h2, gate, wg, wu, wd, final_gain, tm):
    n, d = x1.shape
    ne, _, df = wg.shape
    rb = MOE_ROW_BLOCK
    tmp = tm + N_GROUPS * rb
    row = lambda i, e: (i, 0)
    return pl.pallas_call(
        functools.partial(_moe_sorted_kernel, rb=rb),
        grid=(n // tm, ne // MOE_EXPERTS_PER_STEP),
        in_specs=[
            pl.BlockSpec((tm, d), row),
            pl.BlockSpec((tm, d), row),
            pl.BlockSpec((tm, ROUTER_LANES), row),
            pl.BlockSpec((MOE_EXPERTS_PER_STEP, d, df), lambda i, e: (e, 0, 0)),
            pl.BlockSpec((MOE_EXPERTS_PER_STEP, d, df), lambda i, e: (e, 0, 0)),
            pl.BlockSpec((MOE_EXPERTS_PER_STEP, df, d), lambda i, e: (e, 0, 0)),
            pl.BlockSpec((1, d), lambda i, e: (0, 0)),
        ],
        out_specs=pl.BlockSpec((tm, d), row),
        out_shape=jax.ShapeDtypeStruct((n, d), F32),
        scratch_shapes=[
            pltpu.VMEM((tm, tmp), BF16),
            pltpu.VMEM((tmp, d), BF16),
            pltpu.VMEM((tmp, d), F32),
            pltpu.VMEM((tmp, ROUTER_LANES), F32),
            pltpu.SMEM((2 * N_GROUPS,), jnp.int32),
        ],
        compiler_params=_cparams("parallel", "arbitrary"),
        name="moe_sorted",
    )(x1, h2, gate, wg, wu, wd, final_gain.reshape(1, d))


def _tile(n, pref):
    return pref if n % pref == 0 else n


def _router_params(w_rg, b_rg, w_re, b_re):
    d = w_rg.shape[0]
    pad = ROUTER_LANES - N_GROUPS - N_EXPERTS
    w = jnp.concatenate([w_rg, w_re, jnp.zeros((d, pad), F32)], axis=1)
    hi = w.astype(BF16)
    lo = (w - hi.astype(F32)).astype(BF16)
    b = jnp.concatenate([b_rg, b_re, jnp.zeros((pad,), F32)]).reshape(1, ROUTER_LANES)
    return jnp.stack([hi, lo]), b


def _ffn(x2d, oa, ob, sb_gain, w_out_bf16, ffn_gain, w_router_hl, b_router, wg, wu, wd, final_gain):
    n = x2d.shape[0]
    x1, h2, gate = _post(x2d, oa, ob, sb_gain, w_out_bf16, ffn_gain, w_router_hl, b_router,
                         _tile(n, 512))
    if n % MOE_SORT_TILE == 0:
        return _moe_sorted(x1, h2, gate, wg, wu, wd, final_gain, MOE_SORT_TILE)
    return _moe(x1, h2, gate, wg, wu, wd, final_gain, n)


def kernel(x_prompt, x_sample, cache_k, cache_v, state_hgrn, page_table, norm_attn, w_in,
           sb_logit_bias, sb_norm, hg_lb_logits, hg_norm, w_out, norm_ffn, w_router_group,
           b_router_group, w_router_expert, b_router_expert, w_exp_gate, w_exp_up, w_exp_down,
           norm_final):
    depth = w_in.shape[0]
    assert depth == 1, "single-layer trunk"
    l = 0
    bp, t, d = x_prompt.shape
    db, tn, _ = x_sample.shape
    nh, hd = SB_HEADS, SB_HEAD_DIM
    n_pages = page_table.shape[1]

    w_in_b = w_in[l].astype(BF16)
    w_out_b = w_out[l].astype(BF16)
    wr_hl, b_r = _router_params(w_router_group[l], b_router_group[l], w_router_expert[l],
                                b_router_expert[l])
    df = w_exp_gate.shape[-1]
    wg = w_exp_gate[l].reshape(N_EXPERTS, d, df).astype(BF16)
    wu = w_exp_up[l].reshape(N_EXPERTS, d, df).astype(BF16)
    wd = w_exp_down[l].reshape(N_EXPERTS, df, d).astype(BF16)
    bias = sb_logit_bias[l].astype(F32)

    xp = x_prompt.reshape(bp * t, d)
    w3 = 3 * SB_WIDTH
    wq_scaled = w_in[l][:, :SB_WIDTH] * SB_SCALE
    wqkv_t = jnp.concatenate([wq_scaled, w_in[l][:, SB_WIDTH:w3]], axis=1).T.astype(BF16)
    ktp, vtp, qtb, kb, vtb, hgp = _inproj_prompt(
        x_prompt, norm_attn[l], wqkv_t, w_in_b[:, SB_WIDTH:2 * SB_WIDTH], w_in_b[:, w3:],
        SB_QUERY_BLOCK, SB_QUERY_BLOCK)
    oatp = _sb_prompt(qtb, kb, vtb, bias, SB_KEY_BLOCK)
    obp, sp = _hgrn_prompt(hgp, hg_lb_logits, hg_norm[l], l, 512)
    yp = _ffn(xp, oatp, obp.reshape(bp * t, HG_WIDTH), sb_norm[l], w_out_b, norm_ffn[l], wr_hl,
              b_r, wg, wu, wd, norm_final)
    kp = ktp.transpose(0, 3, 1, 2)[None]
    vp = vtp.transpose(0, 3, 1, 2)[None]

    xs = x_sample.reshape(db * tn, d)
    qs, ks, vs, hgs = _inproj(xs, norm_attn[l], w_in_b, 256)
    head_of_lane = jnp.arange(SB_WIDTH) // hd
    onehot = (head_of_lane[None, :] == jnp.arange(nh)[:, None]).astype(F32)
    qbd = (qs.reshape(db, 1, tn, SB_WIDTH) * SB_SCALE) * onehot[None, :, None, :]
    qbd = qbd.reshape(db, nh * tn, SB_WIDTH).astype(BF16)
    bias_rows = jnp.broadcast_to(jnp.repeat(bias, tn)[:, None], (nh * tn, PAGE_SIZE))
    pad = ((0, 0), (0, 8 - tn), (0, 0))
    kn = jnp.pad(ks.reshape(db, tn, SB_WIDTH), pad)
    vn = jnp.pad(vs.reshape(db, tn, SB_WIDTH), pad)
    ck = cache_k[l].transpose(0, 2, 3, 1).reshape(-1, SB_WIDTH, PAGE_SIZE)
    cv = cache_v[l].transpose(0, 2, 3, 1).reshape(-1, SB_WIDTH, PAGE_SIZE)
    o_full = _sb_sample(qbd, bias_rows, kn, vn, ck, cv, page_table.reshape(-1), n_pages, tn, 16)
    oas = jnp.sum(o_full.reshape(db, nh, tn, SB_WIDTH) * onehot[None, :, None, :], axis=1)
    obs, ss = _hgrn_sample(hgs.reshape(db, tn, -1), state_hgrn[l], hg_lb_logits, hg_norm[l], l)
    ys = _ffn(xs, oas.reshape(db * tn, SB_WIDTH), obs.reshape(db * tn, HG_WIDTH), sb_norm[l],
              w_out_b, norm_ffn[l], wr_hl, b_r, wg, wu, wd, norm_final)

    sdt = state_hgrn.dtype
    return (yp.reshape(bp, t, d), ys.reshape(db, tn, d), kp, vp, sp.astype(sdt)[None],
            ks.reshape(1, db, tn, nh, hd), vs.reshape(1, db, tn, nh, hd), ss.astype(sdt)[None])
```
